```python
import math
import jax, jax.numpy as jnp
from jax import lax
import numpy as np

D_MODEL = 1024
BATCH = 8
SEQ = 2048
DEPTH = 2

CTX_LEN = 256
GRID_W = 64
N_BRANCH = 4
BRANCH_WIDTH = 512
A_HEADS = 4
A_HEAD_DIM = 64
A_VALUE_DIM = 2 * A_HEAD_DIM
QK_WIDTH = A_HEADS * 2 * A_HEAD_DIM
ATTN_SCALE = A_HEAD_DIM ** -0.5
AXIS_ROT = A_HEAD_DIM // 2
ROPE_BASE = 10000.0
Q_BLOCK = 128
CONV_WIDTH = 3
FOURIER_GROUPS = 4
FOURIER_GROUP = BRANCH_WIDTH // FOURIER_GROUPS
POOL_WINDOWS = (2, 4, 8, 16)
POOL_GROUPS = len(POOL_WINDOWS)
POOL_GROUP = BRANCH_WIDTH // POOL_GROUPS
D_FF = 4 * D_MODEL
N_MOD = 6
NORM_EPS = 1e-6
IN_SPLITS = (QK_WIDTH, QK_WIDTH, BRANCH_WIDTH, BRANCH_WIDTH, BRANCH_WIDTH, BRANCH_WIDTH, BRANCH_WIDTH, BRANCH_WIDTH)
IN_WIDTH = sum(IN_SPLITS)

kernel_name = "hybrid_parallel_gated_flow_block"

F32 = jnp.float32


def rms_norm(x, gain):
    xf = x.astype(F32)
    y = xf * lax.rsqrt(jnp.mean(xf * xf, axis=-1, keepdims=True) + NORM_EPS)
    return (y * gain.astype(F32)).astype(x.dtype)


def adaln(cond, w_mod, b_mod):
    return jnp.split(jax.nn.silu(cond) @ w_mod + b_mod, N_MOD, axis=-1)


def modulate(x, gain, shift, scale):
    return rms_norm(x, gain) * (1.0 + scale) + shift


def split_in(u):
    offsets = [int(v) for v in np.cumsum(IN_SPLITS)[:-1]]
    return jnp.split(u, offsets, axis=-1)


def heads_qk(t):
    return t.reshape(t.shape[0], t.shape[1], A_HEADS, 2, A_HEAD_DIM)


def heads_v(t):
    return t.reshape(t.shape[0], t.shape[1], A_HEADS, A_VALUE_DIM)


def axial_rope_tables(L):
    rows = L // GRID_W
    row = jnp.repeat(jnp.arange(rows, dtype=F32), GRID_W)
    col = jnp.tile(jnp.arange(GRID_W, dtype=F32), rows)
    inv = ROPE_BASE ** (-jnp.arange(AXIS_ROT // 2, dtype=F32) * 2.0 / AXIS_ROT)
    ang = jnp.stack([row[:, None] * inv, col[:, None] * inv], axis=1)
    return jnp.cos(ang), jnp.sin(ang)


def apply_axial_rope(x, cos, sin):
    xs = x.reshape(*x.shape[:-1], 2, 2, AXIS_ROT // 2)
    x1, x2 = xs[..., 0, :], xs[..., 1, :]
    cb = cos.astype(x.dtype)[None, :, None, None, :, :]
    sb = sin.astype(x.dtype)[None, :, None, None, :, :]
    out = jnp.stack([x1 * cb - x2 * sb, x2 * cb + x1 * sb], axis=-2)
    return out.reshape(x.shape)


def diff_attention(q, k, v, lam):
    s = jnp.einsum('bqhmd,bkhmd->bhmqk', q, k).astype(F32) * ATTN_SCALE
    p = jax.nn.softmax(s, axis=-1)
    a = p[:, :, 0] - lam * p[:, :, 1]
    return jnp.einsum('bhqk,bkhd->bqhd', a.astype(v.dtype), v)


def blocked_diff_attention(q, k, v, lam):
    B, L = q.shape[0], q.shape[1]
    nb = L // Q_BLOCK
    qb = jnp.moveaxis(q.reshape(B, nb, Q_BLOCK, *q.shape[2:]), 1, 0)
    ob = lax.map(lambda qi: diff_attention(qi, k, v, lam), qb)
    return jnp.moveaxis(ob, 0, 1).reshape(B, L, A_HEADS, A_VALUE_DIM)


def short_conv_mix(bg, cg, xs, conv_w, conv_b):
    L = xs.shape[1]
    z = cg * xs
    pad = CONV_WIDTH // 2
    zp = jnp.pad(z, ((0, 0), (pad, pad), (0, 0)))
    y = sum(zp[:, j:j + L] * conv_w[j] for j in range(CONV_WIDTH)) + conv_b
    return bg * y


def fourier_mix(f):
    B, L, _ = f.shape
    fg = f.reshape(B, L, FOURIER_GROUPS, FOURIER_GROUP).astype(F32)
    y = jnp.fft.fftn(fg, axes=(1, 3), norm='ortho').real
    return y.astype(f.dtype).reshape(B, L, BRANCH_WIDTH)


def multiscale_pool(p, w_pool, pool_scale):
    B, L, _ = p.shape
    pg = p.reshape(B, L, POOL_GROUPS, POOL_GROUP).astype(F32)
    cs = jnp.concatenate([jnp.zeros_like(pg[:, :1]), jnp.cumsum(pg, axis=1)], axis=1)
    t = jnp.arange(L, dtype=jnp.int32)[:, None]
    w = jnp.array(POOL_WINDOWS, dtype=jnp.int32)[None, :]
    lo = jnp.clip(t - w // 2, 0, L)
    hi = jnp.clip(t - w // 2 + w, 0, L)
    gidx = jnp.arange(POOL_GROUPS, dtype=jnp.int32)[None, :]
    s = cs[:, hi, gidx, :] - cs[:, lo, gidx, :]
    mean = s / (hi - lo).astype(F32)[None, :, :, None]
    pooled = (mean - pg).astype(p.dtype)
    y = jnp.einsum('blgc,gcd->blgd', pooled, w_pool) * pool_scale.reshape(POOL_GROUPS, POOL_GROUP)
    return y.reshape(B, L, BRANCH_WIDTH)


def mixer_merge(h, attn, local_parts, subln_g, lam_init, conv_w, conv_b, w_pool, pool_scale,
                w_gate, b_gate, w_br, w_out):
    B, L, _ = h.shape
    bg, cg, xs, f, p = local_parts
    y_attn = (rms_norm(attn, subln_g) * (1.0 - lam_init)).reshape(B, L, BRANCH_WIDTH)
    y_conv = short_conv_mix(bg, cg, xs, conv_w, conv_b)
    y_four = fourier_mix(f)
    y_pool = multiscale_pool(p, w_pool, pool_scale)
    ys = jnp.stack([y_attn, y_conv, y_four, y_pool], axis=2)
    proj = jnp.einsum('blnw,nwd->blnd', ys, w_br)
    gates = jax.nn.sigmoid((h @ w_gate + b_gate).astype(F32)).astype(h.dtype)
    gates = gates.reshape(B, L, N_BRANCH, D_MODEL)
    return jnp.einsum('blnd,blnd->bld', gates, proj) @ w_out


def ffn(h, w1, b1, w2, b2):
    return jnp.square(jax.nn.relu(h @ w1 + b1)) @ w2 + b2


def setup_inputs(seed: int = 0) -> dict:
    key = jax.random.key(seed)
    ks = jax.random.split(key, 32)

    def nrm(k, shape, scale):
        return jax.random.normal(k, shape, F32) * scale

    return {
        "x": nrm(ks[0], (BATCH, SEQ, D_MODEL), 1.0),
        "c": nrm(ks[1], (BATCH, D_MODEL), 1.0),
        "ctx": nrm(ks[2], (BATCH, CTX_LEN, D_MODEL), 1.0),
        "c_ctx": nrm(ks[3], (D_MODEL,), 1.0),
        "w_mod": nrm(ks[4], (DEPTH, D_MODEL, N_MOD * D_MODEL), 0.5 * D_MODEL ** -0.5),
        "b_mod": nrm(ks[5], (DEPTH, N_MOD * D_MODEL), 0.02),
        "norm1_g": 1.0 + nrm(ks[6], (DEPTH, D_MODEL), 0.02),
        "w_in": nrm(ks[7], (DEPTH, D_MODEL, IN_WIDTH), D_MODEL ** -0.5),
        "lam_q1": nrm(ks[8], (DEPTH, A_HEAD_DIM), 0.1),
        "lam_k1": nrm(ks[9], (DEPTH, A_HEAD_DIM), 0.1),
        "lam_q2": nrm(ks[10], (DEPTH, A_HEAD_DIM), 0.1),
        "lam_k2": nrm(ks[11], (DEPTH, A_HEAD_DIM), 0.1),
        "subln_g": 1.0 + nrm(ks[12], (DEPTH, A_VALUE_DIM), 0.02),
        "conv_w": nrm(ks[13], (DEPTH, CONV_WIDTH, BRANCH_WIDTH), CONV_WIDTH ** -0.5),
        "conv_b": nrm(ks[14], (DEPTH, BRANCH_WIDTH), 0.02),
        "w_pool": nrm(ks[15], (DEPTH, POOL_GROUPS, POOL_GROUP, POOL_GROUP), POOL_GROUP ** -0.5),
        "pool_scale": 1.0 + nrm(ks[16], (DEPTH, BRANCH_WIDTH), 0.02),
        "w_gate": nrm(ks[17], (DEPTH, D_MODEL, N_BRANCH * D_MODEL), D_MODEL ** -0.5),
        "b_gate": nrm(ks[18], (DEPTH, N_BRANCH * D_MODEL), 0.02),
        "w_br": nrm(ks[19], (DEPTH, N_BRANCH, BRANCH_WIDTH, D_MODEL), BRANCH_WIDTH ** -0.5),
        "w_out": nrm(ks[20], (DEPTH, D_MODEL, D_MODEL), D_MODEL ** -0.5),
        "norm2_g": 1.0 + nrm(ks[21], (DEPTH, D_MODEL), 0.02),
        "w_ff1": nrm(ks[22], (DEPTH, D_MODEL, D_FF), D_MODEL ** -0.5),
        "b_ff1": nrm(ks[23], (DEPTH, D_FF), 0.02),
        "w_ff2": nrm(ks[24], (DEPTH, D_FF, D_MODEL), D_FF ** -0.5),
        "b_ff2": nrm(ks[25], (DEPTH, D_MODEL), 0.02),
        "final_g": 1.0 + nrm(ks[26], (D_MODEL,), 0.02),
    }


def reference(x, c, ctx, c_ctx, w_mod, b_mod, norm1_g, w_in, lam_q1, lam_k1, lam_q2, lam_k2,
              subln_g, conv_w, conv_b, w_pool, pool_scale, w_gate, b_gate, w_br, w_out,
              norm2_g, w_ff1, b_ff1, w_ff2, b_ff2, final_g):
    L = x.shape[1]
    cos, sin = axial_rope_tables(L)
    for l in range(DEPTH):
        last = l == DEPTH - 1
        lam_init = 0.8 - 0.6 * math.exp(-0.3 * l)
        lam = (jnp.exp(jnp.sum(lam_q1[l].astype(F32) * lam_k1[l].astype(F32)))
               - jnp.exp(jnp.sum(lam_q2[l].astype(F32) * lam_k2[l].astype(F32))) + lam_init)
        sh1, sc1, g1, sh2, sc2, g2 = adaln(c[:, None, :], w_mod[l], b_mod[l])
        csh1, csc1, cg1, csh2, csc2, cg2 = adaln(c_ctx, w_mod[l], b_mod[l])

        hx = modulate(x, norm1_g[l], sh1, sc1)
        hc = modulate(ctx, norm1_g[l], csh1, csc1)
        qx, kx, vx, *local_x = split_in(hx @ w_in[l])
        if last:
            kc, vc = jnp.split(hc @ w_in[l][:, QK_WIDTH:2 * QK_WIDTH + BRANCH_WIDTH], [QK_WIDTH], axis=-1)
        else:
            qc, kc, vc, *local_c = split_in(hc @ w_in[l])
        kc = heads_qk(kc)
        vc = heads_v(vc)
        qx = apply_axial_rope(heads_qk(qx), cos, sin)
        kx = apply_axial_rope(heads_qk(kx), cos, sin)
        k_all = jnp.concatenate([kc, kx], axis=1)
        v_all = jnp.concatenate([vc, heads_v(vx)], axis=1)
        ax = blocked_diff_attention(qx, k_all, v_all, lam)
        x = x + g1 * mixer_merge(hx, ax, local_x, subln_g[l], lam_init, conv_w[l], conv_b[l],
                                 w_pool[l], pool_scale[l], w_gate[l], b_gate[l], w_br[l], w_out[l])
        x = x + g2 * ffn(modulate(x, norm2_g[l], sh2, sc2), w_ff1[l], b_ff1[l], w_ff2[l], b_ff2[l])

        if not last:
            ac = diff_attention(heads_qk(qc), kc, vc, lam)
            ctx = ctx + cg1 * mixer_merge(hc, ac, local_c, subln_g[l], lam_init, conv_w[l], conv_b[l],
                                          w_pool[l], pool_scale[l], w_gate[l], b_gate[l], w_br[l], w_out[l])
            ctx = ctx + cg2 * ffn(modulate(ctx, norm2_g[l], csh2, csc2), w_ff1[l], b_ff1[l], w_ff2[l], b_ff2[l])
    return rms_norm(x, final_g)
```

```python
import functools
import math

import jax
import jax.numpy as jnp
from jax import lax
from jax.experimental import pallas as pl
from jax.experimental.pallas import tpu as pltpu

F32 = jnp.float32
BF16 = jnp.bfloat16

D_MODEL = 1024
CTX_LEN = 256
GRID_W = 64
N_BRANCH = 4
BRANCH_WIDTH = 512
A_HEADS = 4
A_HEAD_DIM = 64
HEAD_COLS = 2 * A_HEAD_DIM
AXIS_ROT = A_HEAD_DIM // 2
ROPE_BASE = 10000.0
ATTN_SCALE = A_HEAD_DIM ** -0.5
CONV_WIDTH = 3
FOURIER_GROUPS = 4
FOURIER_GROUP = BRANCH_WIDTH // FOURIER_GROUPS
POOL_WINDOWS = (2, 4, 8, 16)
POOL_GROUP = BRANCH_WIDTH // len(POOL_WINDOWS)
N_MOD = 6
NORM_EPS = 1e-6
IN_WIDTH = 8 * BRANCH_WIDTH
D_FF = 4 * D_MODEL

TOKEN_TILE = 256
HALO = 16
COND_ROWS = 16
LANES = 128
VMEM_LIMIT = 52 * 1024 * 1024


def _dot(a, b):
    return jnp.dot(a, b, preferred_element_type=F32)


def _rms(x):
    return x * lax.rsqrt(jnp.mean(x * x, axis=-1, keepdims=True) + NORM_EPS)


def _params(*sem):
    return pltpu.CompilerParams(dimension_semantics=sem, vmem_limit_bytes=VMEM_LIMIT)


def _const_spec(shape):
    zeros = (0,) * len(shape)
    return pl.BlockSpec(shape, lambda *_: zeros)


def _adaln_kernel(cond_ref, w_ref, b_ref, o_ref):
    c = cond_ref[...]
    s = c * jax.nn.sigmoid(c)
    o_ref[...] = _dot(s.astype(BF16), w_ref[...].astype(BF16)) + b_ref[...]


def _adaln(cond, w_mod, b_mod):
    depth = w_mod.shape[0]
    return pl.pallas_call(
        _adaln_kernel,
        grid=(depth, N_MOD),
        in_specs=[
            pl.BlockSpec((COND_ROWS, D_MODEL), lambda l, j: (0, 0)),
            pl.BlockSpec((None, D_MODEL, D_MODEL), lambda l, j: (l, 0, j)),
            pl.BlockSpec((None, 1, D_MODEL), lambda l, j: (l, 0, j)),
        ],
        out_specs=pl.BlockSpec((None, COND_ROWS, D_MODEL), lambda l, j: (l, 0, j)),
        out_shape=jax.ShapeDtypeStruct((depth, COND_ROWS, N_MOD * D_MODEL), F32),
        compiler_params=_params("arbitrary", "arbitrary"),
        name="adaln",
    )(cond, w_mod, b_mod.reshape(depth, 1, N_MOD * D_MODEL))


def _inproj_kernel(x_ref, mod_ref, g_ref, w_ref, cos_ref, sup_ref, sdn_ref, u_ref, h_ref):
    x = x_ref[...]
    shift = mod_ref[:, 0:D_MODEL]
    scale = mod_ref[:, D_MODEL:2 * D_MODEL]
    h = _rms(x) * g_ref[...] * (1.0 + scale) + shift
    hb = h.astype(BF16)
    h_ref[...] = hb
    for j in range(IN_WIDTH // BRANCH_WIDTH):
        cols = slice(j * BRANCH_WIDTH, (j + 1) * BRANCH_WIDTH)
        u = _dot(hb, w_ref[:, cols])
        if j < 2:
            parts = []
            for c in range(BRANCH_WIDTH // LANES):
                uc = u[:, c * LANES:(c + 1) * LANES]
                up = pltpu.roll(uc, LANES - AXIS_ROT // 2, axis=1)
                dn = pltpu.roll(uc, AXIS_ROT // 2, axis=1)
                parts.append(uc * cos_ref[...] + up * sup_ref[...] + dn * sdn_ref[...])
            u = jnp.concatenate(parts, axis=1)
            if j == 0:
                u = u * ATTN_SCALE
        u_ref[:, cols] = u.astype(BF16)


def _inproj(xs, modsel, gain, w_in, rope):
    b, t, _ = xs.shape
    nt = t // TOKEN_TILE
    tile = lambda bb, i: (bb, i, 0)
    rope_spec = pl.BlockSpec((TOKEN_TILE, LANES), lambda bb, i: (i, 0))
    return pl.pallas_call(
        _inproj_kernel,
        grid=(b, nt),
        in_specs=[
            pl.BlockSpec((None, TOKEN_TILE, D_MODEL), tile),
            pl.BlockSpec((None, None, 1, N_MOD * D_MODEL), lambda bb, i: (bb, jnp.minimum(i, 1), 0, 0)),
            _const_spec((1, D_MODEL)),
            _const_spec((D_MODEL, IN_WIDTH)),
            rope_spec, rope_spec, rope_spec,
        ],
        out_specs=[
            pl.BlockSpec((None, TOKEN_TILE, IN_WIDTH), tile),
            pl.BlockSpec((None, TOKEN_TILE, D_MODEL), tile),
        ],
        out_shape=[
            jax.ShapeDtypeStruct((b, t, IN_WIDTH), BF16),
            jax.ShapeDtypeStruct((b, t, D_MODEL), BF16),
        ],
        compiler_params=_params("parallel", "arbitrary"),
        name="inproj",
    )(xs, modsel, gain, w_in, *rope)


def _attn_kernel(q_ref, k_ref, v_ref, lam_ref, g_ref, o_ref, *, lam_init, ctx_tile):
    lv = lam_ref[...]
    lam = (jnp.exp(jnp.sum(lv[0:1] * lv[1:2], axis=1, keepdims=True))
           - jnp.exp(jnp.sum(lv[2:3] * lv[3:4], axis=1, keepdims=True)) + lam_init)

    def run(n_keys):
        q = q_ref[...]
        lane = lax.broadcasted_iota(jnp.int32, q.shape, 1)
        zero = jnp.zeros_like(q)
        k = k_ref[0:n_keys, :]
        v = v_ref[0:n_keys, :]

        def softmax_map(qm):
            s = lax.dot_general(qm, k, (((1,), (1,)), ((), ())), preferred_element_type=F32)
            p = jnp.exp(s - jnp.max(s, axis=-1, keepdims=True))
            return p * (1.0 / jnp.sum(p, axis=-1, keepdims=True))

        a = (softmax_map(jnp.where(lane < A_HEAD_DIM, q, zero))
             - lam * softmax_map(jnp.where(lane >= A_HEAD_DIM, q, zero)))
        o = _dot(a.astype(BF16), v)
        o_ref[...] = (_rms(o) * g_ref[...] * (1.0 - lam_init)).astype(BF16)

    if ctx_tile:
        pl.when(pl.program_id(2) == 0)(lambda: run(CTX_LEN))
        pl.when(pl.program_id(2) > 0)(lambda: run(k_ref.shape[0]))
    else:
        run(k_ref.shape[0])


def _attention(u, lamvec, subln_g, lam_init, ctx_tile):
    b, t, _ = u.shape
    first = 0 if ctx_tile else 1
    nq = t // TOKEN_TILE - first
    q_map = lambda bb, h, i: (bb, i + first, h)
    return pl.pallas_call(
        functools.partial(_attn_kernel, lam_init=lam_init, ctx_tile=ctx_tile),
        grid=(b, A_HEADS, nq),
        in_specs=[
            pl.BlockSpec((None, TOKEN_TILE, HEAD_COLS), q_map),
            pl.BlockSpec((None, t, HEAD_COLS), lambda bb, h, i: (bb, 0, A_HEADS + h)),
            pl.BlockSpec((None, t, HEAD_COLS), lambda bb, h, i: (bb, 0, 2 * A_HEADS + h)),
            _const_spec((8, LANES)),
            _const_spec((1, HEAD_COLS)),
        ],
        out_specs=pl.BlockSpec((None, TOKEN_TILE, HEAD_COLS), q_map),
        out_shape=jax.ShapeDtypeStruct((b, t, BRANCH_WIDTH), BF16),
        compiler_params=_params("parallel", "arbitrary", "arbitrary"),
        name="diff_attention",
    )(u, u, u, lamvec, subln_g)


def _fourier_kernel(f_ref, wx_ref, wc_ref, cs_ref, o_ref, z_ref, *, ctx_tile):
    first = 0 if ctx_tile else 1
    i = pl.program_id(1) + first
    seq = f_ref.shape[0] - CTX_LEN

    def channel_dft(rows):
        zz = _dot(f_ref[rows, :], cs_ref[...])
        return zz[:, :BRANCH_WIDTH].astype(BF16), zz[:, BRANCH_WIDTH:].astype(BF16)

    if ctx_tile:
        @pl.when(i == 0)
        def _():
            zc, zs = channel_dft(slice(0, CTX_LEN))
            o_ref[...] = (_dot(wc_ref[:, :CTX_LEN], zc) + _dot(wc_ref[:, CTX_LEN:], zs)).astype(BF16)

    @pl.when(i == 1)
    def _():
        for c in range(seq // TOKEN_TILE):
            zc, zs = channel_dft(slice(CTX_LEN + c * TOKEN_TILE, CTX_LEN + (c + 1) * TOKEN_TILE))
            z_ref[c * TOKEN_TILE:(c + 1) * TOKEN_TILE, :] = zc
            z_ref[seq + c * TOKEN_TILE:seq + (c + 1) * TOKEN_TILE, :] = zs

    @pl.when(i >= 1)
    def _():
        o_ref[...] = _dot(wx_ref[...], z_ref[...]).astype(BF16)


def _fourier(u, wx, wc, cs, ctx_tile):
    b, t, _ = u.shape
    seq = t - CTX_LEN
    first = 0 if ctx_tile else 1
    nt = t // TOKEN_TILE - first
    return pl.pallas_call(
        functools.partial(_fourier_kernel, ctx_tile=ctx_tile),
        grid=(b, nt),
        in_specs=[
            pl.BlockSpec((None, t, BRANCH_WIDTH), lambda bb, i: (bb, 0, 6)),
            pl.BlockSpec((TOKEN_TILE, 2 * seq), lambda bb, i: (jnp.maximum(i + first - 1, 0), 0)),
            _const_spec((CTX_LEN, 2 * CTX_LEN)),
            _const_spec((BRANCH_WIDTH, 2 * BRANCH_WIDTH)),
        ],
        out_specs=pl.BlockSpec((None, TOKEN_TILE, BRANCH_WIDTH), lambda bb, i: (bb, i + first, 0)),
        out_shape=jax.ShapeDtypeStruct((b, t, BRANCH_WIDTH), BF16),
        scratch_shapes=[pltpu.VMEM((2 * seq, BRANCH_WIDTH), BF16)],
        compiler_params=_params("parallel", "arbitrary"),
        name="fourier",
    )(u, wx, wc, cs)


def _merge_kernel(x_ref, h_ref, bg_ref, cg_ref, xs_ref, p_ref,
                  cgp_ref, xsp_ref, pp_ref, cgn_ref, xsn_ref, pn_ref,
                  ya_ref, yf_ref, mod_ref, convw_ref, convb_ref, wpool_ref, pscale_ref,
                  wgate_ref, bgate_ref, wbr_ref, wout_ref, o_ref, zbuf, pbuf, *, ctx_tile, n_tiles):
    first = 0 if ctx_tile else 1
    i = pl.program_id(1) + first
    tm = TOKEN_TILE
    prev_ok = i >= 2
    next_ok = jnp.logical_and(i >= 1, i < n_tiles - 1)

    z = cg_ref[...].astype(F32) * xs_ref[...].astype(F32)
    zp = cgp_ref[...].astype(F32) * xsp_ref[...].astype(F32)
    zn = cgn_ref[...].astype(F32) * xsn_ref[...].astype(F32)
    zbuf[0:HALO, :] = jnp.where(prev_ok, zp, 0.0)
    zbuf[HALO:HALO + tm, :] = z
    zbuf[HALO + tm:, :] = jnp.where(next_ok, zn, 0.0)
    conv = (zbuf[HALO - 1:HALO - 1 + tm, :] * convw_ref[0:1, :] + z * convw_ref[1:2, :]
            + zbuf[HALO + 1:HALO + 1 + tm, :] * convw_ref[2:3, :] + convb_ref[...])
    y_conv = (bg_ref[...].astype(F32) * conv).astype(BF16)

    p = p_ref[...].astype(F32)
    pbuf[0:HALO, :] = jnp.where(prev_ok, pp_ref[...].astype(F32), 0.0)
    pbuf[HALO:HALO + tm, :] = p
    pbuf[HALO + tm:, :] = jnp.where(next_ok, pn_ref[...].astype(F32), 0.0)
    pos = (lax.broadcasted_iota(jnp.int32, (tm, POOL_GROUP), 0)
           + jnp.where(i == 0, 0, (i - 1) * tm))
    seq_len = jnp.where(i == 0, CTX_LEN, (n_tiles - 1) * tm)
    pool_parts = []
    for g, w in enumerate(POOL_WINDOWS):
        cols = slice(g * POOL_GROUP, (g + 1) * POOL_GROUP)
        acc = pbuf[HALO - w // 2:HALO - w // 2 + tm, cols]
        for d in range(1, w):
            acc = acc + pbuf[HALO - w // 2 + d:HALO - w // 2 + d + tm, cols]
        lo = jnp.maximum(pos - w // 2, 0)
        hi = jnp.minimum(pos - w // 2 + w, seq_len)
        pooled = (acc / (hi - lo).astype(F32) - p[:, cols]).astype(BF16)
        pool_parts.append(_dot(pooled, wpool_ref[g]) * pscale_ref[:, cols])
    y_pool = jnp.concatenate(pool_parts, axis=1).astype(BF16)

    ys = (ya_ref[...], y_conv, yf_ref[...], y_pool)
    hb = h_ref[...]
    acc = jnp.zeros((tm, D_MODEL), F32)
    for n in range(N_BRANCH):
        cols = slice(n * D_MODEL, (n + 1) * D_MODEL)
        gate = jax.nn.sigmoid(_dot(hb, wgate_ref[:, cols]) + bgate_ref[:, cols])
        acc = acc + gate * _dot(ys[n], wbr_ref[n])
    out = _dot(acc.astype(BF16), wout_ref[...])
    g1 = mod_ref[:, 2 * D_MODEL:3 * D_MODEL]
    o_ref[...] = x_ref[...] + g1 * out


def _merge(xs, h, u, ya, yf, modsel, conv_w, conv_b, w_pool, pool_scale, w_gate, b_gate, w_br, w_out,
           ctx_tile):
    b, t, _ = xs.shape
    n_tiles = t // TOKEN_TILE
    first = 0 if ctx_tile else 1
    hb = TOKEN_TILE // HALO
    tile = lambda bb, i: (bb, i + first, 0)
    col = lambda j: pl.BlockSpec((None, TOKEN_TILE, BRANCH_WIDTH), lambda bb, i: (bb, i + first, j))
    prev = lambda j: pl.BlockSpec(
        (None, HALO, BRANCH_WIDTH), lambda bb, i: (bb, jnp.maximum((i + first) * hb - 1, 0), j))
    nxt = lambda j: pl.BlockSpec(
        (None, HALO, BRANCH_WIDTH), lambda bb, i: (bb, jnp.minimum((i + first + 1) * hb, t // HALO - 1), j))
    wide = pl.BlockSpec((None, TOKEN_TILE, D_MODEL), tile)
    branch = pl.BlockSpec((None, TOKEN_TILE, BRANCH_WIDTH), tile)
    return pl.pallas_call(
        functools.partial(_merge_kernel, ctx_tile=ctx_tile, n_tiles=n_tiles),
        grid=(b, n_tiles - first),
        in_specs=[
            wide, wide, col(3), col(4), col(5), col(7),
            prev(4), prev(5), prev(7), nxt(4), nxt(5), nxt(7),
            branch, branch,
            pl.BlockSpec((None, None, 1, N_MOD * D_MODEL), lambda bb, i: (bb, jnp.minimum(i + first, 1), 0, 0)),
            _const_spec((CONV_WIDTH, BRANCH_WIDTH)),
            _const_spec((1, BRANCH_WIDTH)),
            _const_spec((len(POOL_WINDOWS), POOL_GROUP, POOL_GROUP)),
            _const_spec((1, BRANCH_WIDTH)),
            _const_spec((D_MODEL, N_BRANCH * D_MODEL)),
            _const_spec((1, N_BRANCH * D_MODEL)),
            _const_spec((N_BRANCH, BRANCH_WIDTH, D_MODEL)),
            _const_spec((D_MODEL, D_MODEL)),
        ],
        out_specs=wide,
        out_shape=jax.ShapeDtypeStruct(xs.shape, F32),
        scratch_shapes=[
            pltpu.VMEM((TOKEN_TILE + 2 * HALO, BRANCH_WIDTH), F32),
            pltpu.VMEM((TOKEN_TILE + 2 * HALO, BRANCH_WIDTH), F32),
        ],
        compiler_params=_params("parallel", "arbitrary"),
        name="merge",
    )(xs, h, u, u, u, u, u, u, u, u, u, u, ya, yf, modsel, conv_w, conv_b, w_pool, pool_scale,
      w_gate, b_gate, w_br, w_out)


def _ffn_kernel(x_ref, mod_ref, g_ref, w1_ref, b1_ref, w2_ref, b2_ref, fg_ref, o_ref, *, final):
    x = x_ref[...]
    shift = mod_ref[:, 3 * D_MODEL:4 * D_MODEL]
    scale = mod_ref[:, 4 * D_MODEL:5 * D_MODEL]
    gate = mod_ref[:, 5 * D_MODEL:6 * D_MODEL]
    hb = (_rms(x) * g_ref[...] * (1.0 + scale) + shift).astype(BF16)
    acc = jnp.zeros(x.shape, F32)
    for c in range(D_FF // D_MODEL):
        cols = slice(c * D_MODEL, (c + 1) * D_MODEL)
        hid = jnp.square(jnp.maximum(_dot(hb, w1_ref[:, cols]) + b1_ref[:, cols], 0.0))
        acc = acc + _dot(hid.astype(BF16), w2_ref[cols, :])
    y = x + gate * (acc + b2_ref[...])
    if final:
        y = _rms(y) * fg_ref[...]
    o_ref[...] = y


def _ffn(xs, modsel, gain, w1, b1, w2, b2, final_g, ctx_tile, final):
    b, t, _ = xs.shape
    first = 0 if ctx_tile else 1
    nt = t // TOKEN_TILE - first
    out_first = 1 if final else 0
    out_t = t - CTX_LEN if final else t
    return pl.pallas_call(
        functools.partial(_ffn_kernel, final=final),
        grid=(b, nt),
        in_specs=[
            pl.BlockSpec((None, TOKEN_TILE, D_MODEL), lambda bb, i: (bb, i + first, 0)),
            pl.BlockSpec((None, None, 1, N_MOD * D_MODEL), lambda bb, i: (bb, jnp.minimum(i + first, 1), 0, 0)),
            _const_spec((1, D_MODEL)),
            _const_spec((D_MODEL, D_FF)),
            _const_spec((1, D_FF)),
            _const_spec((D_FF, D_MODEL)),
            _const_spec((1, D_MODEL)),
            _const_spec((1, D_MODEL)),
        ],
        out_specs=pl.BlockSpec((None, TOKEN_TILE, D_MODEL), lambda bb, i: (bb, i + first - out_first, 0)),
        out_shape=jax.ShapeDtypeStruct((b, out_t, D_MODEL), F32),
        compiler_params=_params("parallel", "arbitrary"),
        name="ffn",
    )(xs, modsel, gain, w1, b1, w2, b2, final_g)


def _rope_tables(seq):
    lane = jnp.arange(LANES, dtype=jnp.int32)
    d = lane % A_HEAD_DIM
    axis = d // AXIS_ROT
    upper = (d % AXIS_ROT) // (AXIS_ROT // 2)
    freq = (d % (AXIS_ROT // 2)).astype(F32)
    inv = ROPE_BASE ** (-freq * 2.0 / AXIS_ROT)
    tok = jnp.arange(seq, dtype=jnp.int32)
    pos = jnp.where(axis[None, :] == 0, (tok // GRID_W)[:, None], (tok % GRID_W)[:, None]).astype(F32)
    ang = pos * inv[None, :]
    cos, sin = jnp.cos(ang), jnp.sin(ang)
    s_up = jnp.where(upper[None, :] == 0, -sin, 0.0)
    s_dn = jnp.where(upper[None, :] == 1, sin, 0.0)
    pad = lambda a, v: jnp.concatenate([jnp.full((CTX_LEN, LANES), v, F32), a], axis=0)
    return pad(cos, 1.0), pad(s_up, 0.0), pad(s_dn, 0.0)


def _dft_cos_sin(n, scale):
    k = jnp.arange(n, dtype=jnp.int32)
    ang = ((k[:, None] * k[None, :]) % n).astype(F32) * (2.0 * math.pi / n)
    return jnp.cos(ang) * scale, jnp.sin(ang) * scale


def _position_dft(n):
    lo = 64
    hi = n // lo
    t = jnp.arange(n, dtype=jnp.int32)

    def tables(k):
        ang = ((k[:, None] * t[None, :]) % n).astype(F32) * (2.0 * math.pi / n)
        return jnp.cos(ang), jnp.sin(ang)

    ca, sa = tables(jnp.arange(hi, dtype=jnp.int32) * lo)
    cb, sb = tables(jnp.arange(lo, dtype=jnp.int32))
    cos = (ca[:, None, :] * cb[None, :, :] - sa[:, None, :] * sb[None, :, :]).reshape(n, n)
    sin = (sa[:, None, :] * cb[None, :, :] + ca[:, None, :] * sb[None, :, :]).reshape(n, n)
    return (jnp.concatenate([cos, -sin], axis=1) * n ** -0.5).astype(BF16)


def kernel(x, c, ctx, c_ctx, w_mod, b_mod, norm1_g, w_in, lam_q1, lam_k1, lam_q2, lam_k2, subln_g, conv_w,
           conv_b, w_pool, pool_scale, w_gate, b_gate, w_br, w_out, norm2_g, w_ff1, b_ff1, w_ff2, b_ff2,
           final_g):
    batch, seq, d_model = x.shape
    depth = w_mod.shape[0]
    assert d_model == D_MODEL and ctx.shape[1] == CTX_LEN == TOKEN_TILE and seq % TOKEN_TILE == 0
    assert batch + 1 <= COND_ROWS

    cond = jnp.concatenate([c, c_ctx[None, :], jnp.zeros((COND_ROWS - batch - 1, D_MODEL), F32)], axis=0)
    mods = _adaln(cond, w_mod, b_mod)

    rope = _rope_tables(seq)
    wx = _position_dft(seq)
    cc, sc = _dft_cos_sin(CTX_LEN, CTX_LEN ** -0.5)
    wc = jnp.concatenate([cc, -sc], axis=1).astype(BF16)
    cg, sg = _dft_cos_sin(FOURIER_GROUP, FOURIER_GROUP ** -0.5)
    eye = jnp.eye(FOURIER_GROUPS, dtype=F32)
    cs = jnp.concatenate([jnp.kron(eye, cg), jnp.kron(eye, sg)], axis=1).astype(BF16)

    xs = jnp.concatenate([ctx, x], axis=1)
    row = lambda a: a.reshape(1, -1)
    for l in range(depth):
        last = l == depth - 1
        lam_init = 0.8 - 0.6 * math.exp(-0.3 * l)
        m = mods[l]
        modsel = jnp.stack([jnp.broadcast_to(m[batch], (batch, N_MOD * D_MODEL)), m[:batch]],
                           axis=1)[:, :, None, :]
        lamvec = jnp.pad(jnp.stack([lam_q1[l], lam_k1[l], lam_q2[l], lam_k2[l]]),
                         ((0, 4), (0, LANES - A_HEAD_DIM)))

        u, h = _inproj(xs, modsel, row(norm1_g[l]), w_in[l].astype(BF16), rope)
        ya = _attention(u, lamvec, row(subln_g[l]), lam_init, ctx_tile=not last)
        yf = _fourier(u, wx, wc, cs, ctx_tile=not last)
        xs = _merge(xs, h, u, ya, yf, modsel, conv_w[l], row(conv_b[l]), w_pool[l].astype(BF16),
                    row(pool_scale[l]), w_gate[l].astype(BF16), row(b_gate[l]), w_br[l].astype(BF16),
                    w_out[l].astype(BF16), ctx_tile=not last)
        xs = _ffn(xs, modsel, row(norm2_g[l]), w_ff1[l].astype(BF16), row(b_ff1[l]),
                  w_ff2[l].astype(BF16), row(b_ff2[l]), row(final_g), ctx_tile=not last, final=last)
    return xs
```

```python
import functools
import math

import jax
import jax.numpy as jnp
from jax import lax
from jax.experimental import pallas as pl
from jax.experimental.pallas import tpu as pltpu

F32 = jnp.float32
BF16 = jnp.bfloat16

D_MODEL = 1024
CTX_LEN = 256
GRID_W = 64
N_BRANCH = 4
BRANCH_WIDTH = 512
A_HEADS = 4
A_HEAD_DIM = 64
HEAD_COLS = 2 * A_HEAD_DIM
AXIS_ROT = A_HEAD_DIM // 2
ROPE_BASE = 10000.0
ATTN_SCALE = A_HEAD_DIM ** -0.5
CONV_WIDTH = 3
FOURIER_GROUPS = 4
FOURIER_GROUP = BRANCH_WIDTH // FOURIER_GROUPS
POOL_WINDOWS = (2, 4, 8, 16)
POOL_GROUP = BRANCH_WIDTH // len(POOL_WINDOWS)
N_MOD = 6
NORM_EPS = 1e-6
IN_WIDTH = 8 * BRANCH_WIDTH
D_FF = 4 * D_MODEL

TOKEN_TILE = 256
HALO = 16
COND_ROWS = 16
LOG2_E = math.log2(math.e)
LANES = 128
VMEM_LIMIT = 52 * 1024 * 1024


def _dot(a, b):
    return jnp.dot(a, b, preferred_element_type=F32)


def _rms(x):
    return x * lax.rsqrt(jnp.mean(x * x, axis=-1, keepdims=True) + NORM_EPS)


def _params(*sem):
    return pltpu.CompilerParams(dimension_semantics=sem, vmem_limit_bytes=VMEM_LIMIT)


def _const_spec(shape):
    zeros = (0,) * len(shape)
    return pl.BlockSpec(shape, lambda *_: zeros)


def _adaln_kernel(cond_ref, w_ref, b_ref, o_ref):
    c = cond_ref[...]
    s = c * jax.nn.sigmoid(c)
    o_ref[...] = _dot(s.astype(BF16), w_ref[...].astype(BF16)) + b_ref[...]


def _adaln(cond, w_mod, b_mod):
    depth = w_mod.shape[0]
    return pl.pallas_call(
        _adaln_kernel,
        grid=(depth, N_MOD),
        in_specs=[
            pl.BlockSpec((COND_ROWS, D_MODEL), lambda l, j: (0, 0)),
            pl.BlockSpec((None, D_MODEL, D_MODEL), lambda l, j: (l, 0, j)),
            pl.BlockSpec((None, 1, D_MODEL), lambda l, j: (l, 0, j)),
        ],
        out_specs=pl.BlockSpec((None, COND_ROWS, D_MODEL), lambda l, j: (l, 0, j)),
        out_shape=jax.ShapeDtypeStruct((depth, COND_ROWS, N_MOD * D_MODEL), F32),
        compiler_params=_params("arbitrary", "arbitrary"),
        name="adaln",
    )(cond, w_mod, b_mod.reshape(depth, 1, N_MOD * D_MODEL))


def _inproj_kernel(x_ref, mod_ref, g_ref, w_ref, cos_ref, sup_ref, sdn_ref, u_ref, h_ref):
    x = x_ref[...]
    shift = mod_ref[:, 0:D_MODEL]
    scale = mod_ref[:, D_MODEL:2 * D_MODEL]
    h = _rms(x) * g_ref[...] * (1.0 + scale) + shift
    hb = h.astype(BF16)
    h_ref[...] = hb
    for j in range(IN_WIDTH // BRANCH_WIDTH):
        cols = slice(j * BRANCH_WIDTH, (j + 1) * BRANCH_WIDTH)
        u = _dot(hb, w_ref[:, cols])
        if j < 2:
            parts = []
            for c in range(BRANCH_WIDTH // LANES):
                uc = u[:, c * LANES:(c + 1) * LANES]
                up = pltpu.roll(uc, LANES - AXIS_ROT // 2, axis=1)
                dn = pltpu.roll(uc, AXIS_ROT // 2, axis=1)
                parts.append(uc * cos_ref[...] + up * sup_ref[...] + dn * sdn_ref[...])
            u = jnp.concatenate(parts, axis=1)
            if j == 0:
                u = u * (ATTN_SCALE * LOG2_E)
        u_ref[:, cols] = u.astype(BF16)


def _inproj(xs, modsel, gain, w_in, rope):
    b, t, _ = xs.shape
    nt = t // TOKEN_TILE
    tile = lambda bb, i: (bb, i, 0)
    rope_spec = pl.BlockSpec((TOKEN_TILE, LANES), lambda bb, i: (i, 0))
    return pl.pallas_call(
        _inproj_kernel,
        grid=(b, nt),
        in_specs=[
            pl.BlockSpec((None, TOKEN_TILE, D_MODEL), tile),
            pl.BlockSpec((None, None, 1, N_MOD * D_MODEL), lambda bb, i: (bb, jnp.minimum(i, 1), 0, 0)),
            _const_spec((1, D_MODEL)),
            _const_spec((D_MODEL, IN_WIDTH)),
            rope_spec, rope_spec, rope_spec,
        ],
        out_specs=[
            pl.BlockSpec((None, TOKEN_TILE, IN_WIDTH), tile),
            pl.BlockSpec((None, TOKEN_TILE, D_MODEL), tile),
        ],
        out_shape=[
            jax.ShapeDtypeStruct((b, t, IN_WIDTH), BF16),
            jax.ShapeDtypeStruct((b, t, D_MODEL), BF16),
        ],
        compiler_params=_params("parallel", "arbitrary"),
        name="inproj",
    )(xs, modsel, gain, w_in, *rope)


def _attn_kernel(q_ref, k_ref, v_ref, lam_ref, g_ref, o_ref, vx_ref, *, lam_init, ctx_tile):
    lv = lam_ref[...]
    lam = (jnp.exp(jnp.sum(lv[0:1] * lv[1:2], axis=1, keepdims=True))
           - jnp.exp(jnp.sum(lv[2:3] * lv[3:4], axis=1, keepdims=True)) + lam_init)
    t = k_ref.shape[0]

    @pl.when(pl.program_id(1) == 0)
    def _():
        for h in range(A_HEADS):
            vx_ref[:, 2 * h * HEAD_COLS:(2 * h + 1) * HEAD_COLS] = v_ref[:, h * HEAD_COLS:(h + 1) * HEAD_COLS]
            vx_ref[:, (2 * h + 1) * HEAD_COLS:(2 * h + 2) * HEAD_COLS] = jnp.ones((t, HEAD_COLS), BF16)

    def run(n_keys):
        lane = lax.broadcasted_iota(jnp.int32, (TOKEN_TILE, HEAD_COLS), 1)
        zero = jnp.zeros((TOKEN_TILE, HEAD_COLS), BF16)

        def scores(chain):
            h, m = divmod(chain, 2)
            cols = slice(h * HEAD_COLS, (h + 1) * HEAD_COLS)
            qm = jnp.where((lane >= A_HEAD_DIM) if m else (lane < A_HEAD_DIM), q_ref[:, cols], zero)
            return lax.dot_general(qm, k_ref[0:n_keys, cols], (((1,), (1,)), ((), ())),
                                   preferred_element_type=F32)

        n_chains = 2 * A_HEADS
        res = []
        s_next = scores(0)
        for chain in range(n_chains):
            s = s_next
            if chain + 1 < n_chains:
                s_next = scores(chain + 1)
            p = jnp.exp2(s - jnp.max(s, axis=-1, keepdims=True)).astype(BF16)
            h = chain // 2
            res.append(_dot(p, vx_ref[0:n_keys, 2 * h * HEAD_COLS:(2 * h + 2) * HEAD_COLS]))
        outs = []
        for h in range(A_HEADS):
            r1, r2 = res[2 * h], res[2 * h + 1]
            o = (r1[:, :HEAD_COLS] / r1[:, HEAD_COLS:] - lam * (r2[:, :HEAD_COLS] / r2[:, HEAD_COLS:]))
            outs.append(_rms(o))
        y = jnp.concatenate(outs, axis=1) * g_ref[...] * (1.0 - lam_init)
        o_ref[...] = y.astype(BF16)

    if ctx_tile:
        pl.when(pl.program_id(1) == 0)(lambda: run(CTX_LEN))
        pl.when(pl.program_id(1) > 0)(lambda: run(t))
    else:
        run(t)


def _attention(u, lamvec, subln_g, lam_init, ctx_tile):
    b, t, _ = u.shape
    first = 0 if ctx_tile else 1
    nq = t // TOKEN_TILE - first
    q_map = lambda bb, i: (bb, i + first, 0)
    return pl.pallas_call(
        functools.partial(_attn_kernel, lam_init=lam_init, ctx_tile=ctx_tile),
        grid=(b, nq),
        in_specs=[
            pl.BlockSpec((None, TOKEN_TILE, BRANCH_WIDTH), q_map),
            pl.BlockSpec((None, t, BRANCH_WIDTH), lambda bb, i: (bb, 0, 1)),
            pl.BlockSpec((None, t, BRANCH_WIDTH), lambda bb, i: (bb, 0, 2)),
            _const_spec((8, LANES)),
            _const_spec((1, BRANCH_WIDTH)),
        ],
        out_specs=pl.BlockSpec((None, TOKEN_TILE, BRANCH_WIDTH), q_map),
        out_shape=jax.ShapeDtypeStruct((b, t, BRANCH_WIDTH), BF16),
        scratch_shapes=[pltpu.VMEM((t, 2 * BRANCH_WIDTH), BF16)],
        compiler_params=_params("parallel", "arbitrary"),
        name="diff_attention",
    )(u, u, u, lamvec, subln_g)


def _fourier_kernel(f_ref, wx_ref, wc_ref, cs_ref, o_ref, z_ref, *, ctx_tile):
    first = 0 if ctx_tile else 1
    i = pl.program_id(1) + first
    seq = f_ref.shape[0] - CTX_LEN

    def channel_dft(rows):
        zz = _dot(f_ref[rows, :], cs_ref[...])
        return zz[:, :BRANCH_WIDTH].astype(BF16), zz[:, BRANCH_WIDTH:].astype(BF16)

    if ctx_tile:
        @pl.when(i == 0)
        def _():
            zc, zs = channel_dft(slice(0, CTX_LEN))
            o_ref[...] = (_dot(wc_ref[:, :CTX_LEN], zc) + _dot(wc_ref[:, CTX_LEN:], zs)).astype(BF16)

    @pl.when(i == 1)
    def _():
        for c in range(seq // TOKEN_TILE):
            zc, zs = channel_dft(slice(CTX_LEN + c * TOKEN_TILE, CTX_LEN + (c + 1) * TOKEN_TILE))
            z_ref[c * TOKEN_TILE:(c + 1) * TOKEN_TILE, :] = zc
            z_ref[seq + c * TOKEN_TILE:seq + (c + 1) * TOKEN_TILE, :] = zs

    @pl.when(i >= 1)
    def _():
        o_ref[...] = _dot(wx_ref[...], z_ref[...]).astype(BF16)


def _fourier(u, wx, wc, cs, ctx_tile):
    b, t, _ = u.shape
    seq = t - CTX_LEN
    first = 0 if ctx_tile else 1
    nt = t // TOKEN_TILE - first
    return pl.pallas_call(
        functools.partial(_fourier_kernel, ctx_tile=ctx_tile),
        grid=(b, nt),
        in_specs=[
            pl.BlockSpec((None, t, BRANCH_WIDTH), lambda bb, i: (bb, 0, 6)),
            pl.BlockSpec((TOKEN_TILE, 2 * seq), lambda bb, i: (jnp.maximum(i + first - 1, 0), 0)),
            _const_spec((CTX_LEN, 2 * CTX_LEN)),
            _const_spec((BRANCH_WIDTH, 2 * BRANCH_WIDTH)),
        ],
        out_specs=pl.BlockSpec((None, TOKEN_TILE, BRANCH_WIDTH), lambda bb, i: (bb, i + first, 0)),
        out_shape=jax.ShapeDtypeStruct((b, t, BRANCH_WIDTH), BF16),
        scratch_shapes=[pltpu.VMEM((2 * seq, BRANCH_WIDTH), BF16)],
        compiler_params=_params("parallel", "arbitrary"),
        name="fourier",
    )(u, wx, wc, cs)


def _merge_kernel(x_ref, h_ref, bg_ref, cg_ref, xs_ref, p_ref,
                  cgp_ref, xsp_ref, pp_ref, cgn_ref, xsn_ref, pn_ref,
                  ya_ref, yf_ref, mod_ref, convw_ref, convb_ref, wpool_ref, pscale_ref,
                  wgate_ref, bgate_ref, wbr_ref, wout_ref, o_ref, zbuf, pbuf, *, ctx_tile, n_tiles):
    first = 0 if ctx_tile else 1
    i = pl.program_id(1) + first
    tm = TOKEN_TILE
    prev_ok = i >= 2
    next_ok = jnp.logical_and(i >= 1, i < n_tiles - 1)

    z = cg_ref[...].astype(F32) * xs_ref[...].astype(F32)
    zp = cgp_ref[...].astype(F32) * xsp_ref[...].astype(F32)
    zn = cgn_ref[...].astype(F32) * xsn_ref[...].astype(F32)
    zbuf[0:HALO, :] = jnp.where(prev_ok, zp, 0.0)
    zbuf[HALO:HALO + tm, :] = z
    zbuf[HALO + tm:, :] = jnp.where(next_ok, zn, 0.0)
    conv = (zbuf[HALO - 1:HALO - 1 + tm, :] * convw_ref[0:1, :] + z * convw_ref[1:2, :]
            + zbuf[HALO + 1:HALO + 1 + tm, :] * convw_ref[2:3, :] + convb_ref[...])
    y_conv = (bg_ref[...].astype(F32) * conv).astype(BF16)

    p = p_ref[...].astype(F32)
    pbuf[0:HALO, :] = jnp.where(prev_ok, pp_ref[...].astype(F32), 0.0)
    pbuf[HALO:HALO + tm, :] = p
    pbuf[HALO + tm:, :] = jnp.where(next_ok, pn_ref[...].astype(F32), 0.0)
    pos = (lax.broadcasted_iota(jnp.int32, (tm, POOL_GROUP), 0)
           + jnp.where(i == 0, 0, (i - 1) * tm))
    seq_len = jnp.where(i == 0, CTX_LEN, (n_tiles - 1) * tm)
    pool_parts = []
    for g, w in enumerate(POOL_WINDOWS):
        cols = slice(g * POOL_GROUP, (g + 1) * POOL_GROUP)
        acc = pbuf[HALO - w // 2:HALO - w // 2 + tm, cols]
        for d in range(1, w):
            acc = acc + pbuf[HALO - w // 2 + d:HALO - w // 2 + d + tm, cols]
        lo = jnp.maximum(pos - w // 2, 0)
        hi = jnp.minimum(pos - w // 2 + w, seq_len)
        pooled = (acc / (hi - lo).astype(F32) - p[:, cols]).astype(BF16)
        pool_parts.append(_dot(pooled, wpool_ref[g]) * pscale_ref[:, cols])
    y_pool = jnp.concatenate(pool_parts, axis=1).astype(BF16)

    ys = (ya_ref[...], y_conv, yf_ref[...], y_pool)
    hb = h_ref[...]
    acc = jnp.zeros((tm, D_MODEL), F32)
    for n in range(N_BRANCH):
        cols = slice(n * D_MODEL, (n + 1) * D_MODEL)
        gate = jax.nn.sigmoid(_dot(hb, wgate_ref[:, cols]) + bgate_ref[:, cols])
        acc = acc + gate * _dot(ys[n], wbr_ref[n])
    out = _dot(acc.astype(BF16), wout_ref[...])
    g1 = mod_ref[:, 2 * D_MODEL:3 * D_MODEL]
    o_ref[...] = x_ref[...] + g1 * out


def _merge(xs, h, u, ya, yf, modsel, conv_w, conv_b, w_pool, pool_scale, w_gate, b_gate, w_br, w_out,
           ctx_tile):
    b, t, _ = xs.shape
    n_tiles = t // TOKEN_TILE
    first = 0 if ctx_tile else 1
    hb = TOKEN_TILE // HALO
    tile = lambda bb, i: (bb, i + first, 0)
    col = lambda j: pl.BlockSpec((None, TOKEN_TILE, BRANCH_WIDTH), lambda bb, i: (bb, i + first, j))
    prev = lambda j: pl.BlockSpec(
        (None, HALO, BRANCH_WIDTH), lambda bb, i: (bb, jnp.maximum((i + first) * hb - 1, 0), j))
    nxt = lambda j: pl.BlockSpec(
        (None, HALO, BRANCH_WIDTH), lambda bb, i: (bb, jnp.minimum((i + first + 1) * hb, t // HALO - 1), j))
    wide = pl.BlockSpec((None, TOKEN_TILE, D_MODEL), tile)
    branch = pl.BlockSpec((None, TOKEN_TILE, BRANCH_WIDTH), tile)
    return pl.pallas_call(
        functools.partial(_merge_kernel, ctx_tile=ctx_tile, n_tiles=n_tiles),
        grid=(b, n_tiles - first),
        in_specs=[
            wide, wide, col(3), col(4), col(5), col(7),
            prev(4), prev(5), prev(7), nxt(4), nxt(5), nxt(7),
            branch, branch,
            pl.BlockSpec((None, None, 1, N_MOD * D_MODEL), lambda bb, i: (bb, jnp.minimum(i + first, 1), 0, 0)),
            _const_spec((CONV_WIDTH, BRANCH_WIDTH)),
            _const_spec((1, BRANCH_WIDTH)),
            _const_spec((len(POOL_WINDOWS), POOL_GROUP, POOL_GROUP)),
            _const_spec((1, BRANCH_WIDTH)),
            _const_spec((D_MODEL, N_BRANCH * D_MODEL)),
            _const_spec((1, N_BRANCH * D_MODEL)),
            _const_spec((N_BRANCH, BRANCH_WIDTH, D_MODEL)),
            _const_spec((D_MODEL, D_MODEL)),
        ],
        out_specs=wide,
        out_shape=jax.ShapeDtypeStruct(xs.shape, F32),
        scratch_shapes=[
            pltpu.VMEM((TOKEN_TILE + 2 * HALO, BRANCH_WIDTH), F32),
            pltpu.VMEM((TOKEN_TILE + 2 * HALO, BRANCH_WIDTH), F32),
        ],
        compiler_params=_params("parallel", "arbitrary"),
        name="merge",
    )(xs, h, u, u, u, u, u, u, u, u, u, u, ya, yf, modsel, conv_w, conv_b, w_pool, pool_scale,
      w_gate, b_gate, w_br, w_out)


def _ffn_kernel(x_ref, mod_ref, g_ref, w1_ref, b1_ref, w2_ref, b2_ref, fg_ref, o_ref, *, final):
    x = x_ref[...]
    shift = mod_ref[:, 3 * D_MODEL:4 * D_MODEL]
    scale = mod_ref[:, 4 * D_MODEL:5 * D_MODEL]
    gate = mod_ref[:, 5 * D_MODEL:6 * D_MODEL]
    hb = (_rms(x) * g_ref[...] * (1.0 + scale) + shift).astype(BF16)
    acc = jnp.zeros(x.shape, F32)
    for c in range(D_FF // D_MODEL):
        cols = slice(c * D_MODEL, (c + 1) * D_MODEL)
        hid = jnp.square(jnp.maximum(_dot(hb, w1_ref[:, cols]) + b1_ref[:, cols], 0.0))
        acc = acc + _dot(hid.astype(BF16), w2_ref[cols, :])
    y = x + gate * (acc + b2_ref[...])
    if final:
        y = _rms(y) * fg_ref[...]
    o_ref[...] = y


def _ffn(xs, modsel, gain, w1, b1, w2, b2, final_g, ctx_tile, final):
    b, t, _ = xs.shape
    first = 0 if ctx_tile else 1
    nt = t // TOKEN_TILE - first
    out_first = 1 if final else 0
    out_t = t - CTX_LEN if final else t
    return pl.pallas_call(
        functools.partial(_ffn_kernel, final=final),
        grid=(b, nt),
        in_specs=[
            pl.BlockSpec((None, TOKEN_TILE, D_MODEL), lambda bb, i: (bb, i + first, 0)),
            pl.BlockSpec((None, None, 1, N_MOD * D_MODEL), lambda bb, i: (bb, jnp.minimum(i + first, 1), 0, 0)),
            _const_spec((1, D_MODEL)),
            _const_spec((D_MODEL, D_FF)),
            _const_spec((1, D_FF)),
            _const_spec((D_FF, D_MODEL)),
            _const_spec((1, D_MODEL)),
            _const_spec((1, D_MODEL)),
        ],
        out_specs=pl.BlockSpec((None, TOKEN_TILE, D_MODEL), lambda bb, i: (bb, i + first - out_first, 0)),
        out_shape=jax.ShapeDtypeStruct((b, out_t, D_MODEL), F32),
        compiler_params=_params("parallel", "arbitrary"),
        name="ffn",
    )(xs, modsel, gain, w1, b1, w2, b2, final_g)


def _rope_tables(seq):
    lane = jnp.arange(LANES, dtype=jnp.int32)
    d = lane % A_HEAD_DIM
    axis = d // AXIS_ROT
    upper = (d % AXIS_ROT) // (AXIS_ROT // 2)
    freq = (d % (AXIS_ROT // 2)).astype(F32)
    inv = ROPE_BASE ** (-freq * 2.0 / AXIS_ROT)
    tok = jnp.arange(seq, dtype=jnp.int32)
    pos = jnp.where(axis[None, :] == 0, (tok // GRID_W)[:, None], (tok % GRID_W)[:, None]).astype(F32)
    ang = pos * inv[None, :]
    cos, sin = jnp.cos(ang), jnp.sin(ang)
    s_up = jnp.where(upper[None, :] == 0, -sin, 0.0)
    s_dn = jnp.where(upper[None, :] == 1, sin, 0.0)
    pad = lambda a, v: jnp.concatenate([jnp.full((CTX_LEN, LANES), v, F32), a], axis=0)
    return pad(cos, 1.0), pad(s_up, 0.0), pad(s_dn, 0.0)


def _dft_cos_sin(n, scale):
    k = jnp.arange(n, dtype=jnp.int32)
    ang = ((k[:, None] * k[None, :]) % n).astype(F32) * (2.0 * math.pi / n)
    return jnp.cos(ang) * scale, jnp.sin(ang) * scale


def _position_dft(n):
    lo = 64
    hi = n // lo
    t = jnp.arange(n, dtype=jnp.int32)

    def tables(k):
        ang = ((k[:, None] * t[None, :]) % n).astype(F32) * (2.0 * math.pi / n)
        return jnp.cos(ang), jnp.sin(ang)

    ca, sa = tables(jnp.arange(hi, dtype=jnp.int32) * lo)
    cb, sb = tables(jnp.arange(lo, dtype=jnp.int32))
    cos = (ca[:, None, :] * cb[None, :, :] - sa[:, None, :] * sb[None, :, :]).reshape(n, n)
    sin = (sa[:, None, :] * cb[None, :, :] + ca[:, None, :] * sb[None, :, :]).reshape(n, n)
    return (jnp.concatenate([cos, -sin], axis=1) * n ** -0.5).astype(BF16)


def kernel(x, c, ctx, c_ctx, w_mod, b_mod, norm1_g, w_in, lam_q1, lam_k1, lam_q2, lam_k2, subln_g, conv_w,
           conv_b, w_pool, pool_scale, w_gate, b_gate, w_br, w_out, norm2_g, w_ff1, b_ff1, w_ff2, b_ff2,
           final_g):
    batch, seq, d_model = x.shape
    depth = w_mod.shape[0]
    assert d_model == D_MODEL and ctx.shape[1] == CTX_LEN == TOKEN_TILE and seq % TOKEN_TILE == 0
    assert batch + 1 <= COND_ROWS

    cond = jnp.concatenate([c, c_ctx[None, :], jnp.zeros((COND_ROWS - batch - 1, D_MODEL), F32)], axis=0)
    mods = _adaln(cond, w_mod, b_mod)

    rope = _rope_tables(seq)
    wx = _position_dft(seq)
    cc, sc = _dft_cos_sin(CTX_LEN, CTX_LEN ** -0.5)
    wc = jnp.concatenate([cc, -sc], axis=1).astype(BF16)
    cg, sg = _dft_cos_sin(FOURIER_GROUP, FOURIER_GROUP ** -0.5)
    eye = jnp.eye(FOURIER_GROUPS, dtype=F32)
    cs = jnp.concatenate([jnp.kron(eye, cg), jnp.kron(eye, sg)], axis=1).astype(BF16)

    xs = jnp.concatenate([ctx, x], axis=1)
    row = lambda a: a.reshape(1, -1)
    for l in range(depth):
        last = l == depth - 1
        lam_init = 0.8 - 0.6 * math.exp(-0.3 * l)
        m = mods[l]
        modsel = jnp.stack([jnp.broadcast_to(m[batch], (batch, N_MOD * D_MODEL)), m[:batch]],
                           axis=1)[:, :, None, :]
        lamvec = jnp.pad(jnp.stack([lam_q1[l], lam_k1[l], lam_q2[l], lam_k2[l]]),
                         ((0, 4), (0, LANES - A_HEAD_DIM)))

        u, h = _inproj(xs, modsel, row(norm1_g[l]), w_in[l].astype(BF16), rope)
        ya = _attention(u, lamvec, row(jnp.tile(subln_g[l], A_HEADS)), lam_init, ctx_tile=not last)
        yf = _fourier(u, wx, wc, cs, ctx_tile=not last)
        xs = _merge(xs, h, u, ya, yf, modsel, conv_w[l], row(conv_b[l]), w_pool[l].astype(BF16),
                    row(pool_scale[l]), w_gate[l].astype(BF16), row(b_gate[l]), w_br[l].astype(BF16),
                    w_out[l].astype(BF16), ctx_tile=not last)
        xs = _ffn(xs, modsel, row(norm2_g[l]), w_ff1[l].astype(BF16), row(b_ff1[l]),
                  w_ff2[l].astype(BF16), row(b_ff2[l]), row(final_g), ctx_tile=not last, final=last)
    return xs
```

```python
import functools
import math

import jax
import jax.numpy as jnp
from jax import lax
from jax.experimental import pallas as pl
from jax.experimental.pallas import tpu as pltpu

F32 = jnp.float32
BF16 = jnp.bfloat16

D_MODEL = 1024
CTX_LEN = 256
GRID_W = 64
N_BRANCH = 4
BRANCH_WIDTH = 512
A_HEADS = 4
A_HEAD_DIM = 64
HEAD_COLS = 2 * A_HEAD_DIM
AXIS_ROT = A_HEAD_DIM // 2
ROPE_BASE = 10000.0
ATTN_SCALE = A_HEAD_DIM ** -0.5
CONV_WIDTH = 3
FOURIER_GROUPS = 4
FOURIER_GROUP = BRANCH_WIDTH // FOURIER_GROUPS
POOL_WINDOWS = (2, 4, 8, 16)
POOL_GROUP = BRANCH_WIDTH // len(POOL_WINDOWS)
N_MOD = 6
NORM_EPS = 1e-6
IN_WIDTH = 8 * BRANCH_WIDTH
D_FF = 4 * D_MODEL

TOKEN_TILE = 256
HALO = 16
COND_ROWS = 16
LANES = 128
SUBLANES = 8
REV_BLOCK = 256
LOG2_E = math.log2(math.e)
VMEM_LIMIT = 52 * 1024 * 1024


def _dot(a, b):
    return jnp.dot(a, b, preferred_element_type=F32)


def _rms(x):
    return x * lax.rsqrt(jnp.mean(x * x, axis=-1, keepdims=True) + NORM_EPS)


def _params(*sem):
    return pltpu.CompilerParams(dimension_semantics=sem, vmem_limit_bytes=VMEM_LIMIT)


def _const_spec(shape):
    zeros = (0,) * len(shape)
    return pl.BlockSpec(shape, lambda *_: zeros)


def _layer_spec(shape, layer):
    zeros = (0,) * len(shape)
    return pl.BlockSpec((None,) + shape, lambda *_: (layer,) + zeros)


def _mod_spec(first):
    return pl.BlockSpec((None, None, 1, N_MOD * D_MODEL), lambda bb, i: (bb, jnp.minimum(i + first, 1), 0, 0))


def _stream_specs(split, width, first):
    if split:
        return [pl.BlockSpec((None, TOKEN_TILE, width), lambda bb, i: (bb, 0, 0)),
                pl.BlockSpec((None, TOKEN_TILE, width), lambda bb, i: (bb, jnp.maximum(i + first - 1, 0), 0))]
    return [pl.BlockSpec((None, TOKEN_TILE, width), lambda bb, i: (bb, i + first, 0))]


def _stream_tile(refs, i):
    if len(refs) == 2:
        return jnp.where(i == 0, refs[0][...], refs[1][...])
    return refs[0][...]


def _adaln_kernel(cond_ref, w_ref, b_ref, o_ref):
    c = cond_ref[...]
    s = c * jax.nn.sigmoid(c)
    o_ref[...] = _dot(s.astype(BF16), w_ref[...].astype(BF16)) + b_ref[...]


def _adaln(cond, w_mod, b_mod):
    depth = w_mod.shape[0]
    return pl.pallas_call(
        _adaln_kernel,
        grid=(depth, N_MOD),
        in_specs=[
            pl.BlockSpec((COND_ROWS, D_MODEL), lambda l, j: (0, 0)),
            pl.BlockSpec((None, D_MODEL, D_MODEL), lambda l, j: (l, 0, j)),
            pl.BlockSpec((None, 1, D_MODEL), lambda l, j: (l, 0, j)),
        ],
        out_specs=pl.BlockSpec((None, COND_ROWS, D_MODEL), lambda l, j: (l, 0, j)),
        out_shape=jax.ShapeDtypeStruct((depth, COND_ROWS, N_MOD * D_MODEL), F32),
        compiler_params=_params("arbitrary", "arbitrary"),
        name="adaln",
    )(cond, w_mod, b_mod.reshape(depth, 1, N_MOD * D_MODEL))


def _inproj_kernel(*refs, n_stream):
    x_refs = refs[:n_stream]
    mod_ref, g_ref, w_ref, cos_ref, sup_ref, sdn_ref, u_ref, h_ref = refs[n_stream:]
    x = _stream_tile(x_refs, pl.program_id(1))
    shift = mod_ref[:, 0:D_MODEL]
    scale = mod_ref[:, D_MODEL:2 * D_MODEL]
    h = _rms(x) * g_ref[...] * (1.0 + scale) + shift
    hb = h.astype(BF16)
    h_ref[...] = hb
    for j in range(IN_WIDTH // BRANCH_WIDTH):
        cols = slice(j * BRANCH_WIDTH, (j + 1) * BRANCH_WIDTH)
        u = _dot(hb, w_ref[:, cols])
        if j < 2:
            parts = []
            for c in range(BRANCH_WIDTH // LANES):
                uc = u[:, c * LANES:(c + 1) * LANES]
                up = pltpu.roll(uc, LANES - AXIS_ROT // 2, axis=1)
                dn = pltpu.roll(uc, AXIS_ROT // 2, axis=1)
                parts.append(uc * cos_ref[...] + up * sup_ref[...] + dn * sdn_ref[...])
            u = jnp.concatenate(parts, axis=1)
            if j == 0:
                u = u * (ATTN_SCALE * LOG2_E)
        u_ref[:, cols] = u.astype(BF16)


def _inproj(streams, modsel, gain, w_in, layer, rope):
    b = streams[0].shape[0]
    t = rope[0].shape[0]
    tile = lambda bb, i: (bb, i, 0)
    rope_spec = pl.BlockSpec((TOKEN_TILE, LANES), lambda bb, i: (i, 0))
    return pl.pallas_call(
        functools.partial(_inproj_kernel, n_stream=len(streams)),
        grid=(b, t // TOKEN_TILE),
        in_specs=_stream_specs(len(streams) == 2, D_MODEL, 0) + [
            _mod_spec(0),
            _const_spec((1, D_MODEL)),
            _layer_spec((D_MODEL, IN_WIDTH), layer),
            rope_spec, rope_spec, rope_spec,
        ],
        out_specs=[
            pl.BlockSpec((None, TOKEN_TILE, IN_WIDTH), tile),
            pl.BlockSpec((None, TOKEN_TILE, D_MODEL), tile),
        ],
        out_shape=[
            jax.ShapeDtypeStruct((b, t, IN_WIDTH), BF16),
            jax.ShapeDtypeStruct((b, t, D_MODEL), BF16),
        ],
        compiler_params=_params("parallel", "arbitrary"),
        name="inproj",
    )(*streams, modsel, gain, w_in, *rope)


def _attn_kernel(q_ref, k_ref, v_ref, lam_ref, g_ref, o_ref, vx_ref, *, lam_init, ctx_tile):
    lv = lam_ref[...]
    lam = (jnp.exp(jnp.sum(lv[0:1] * lv[1:2], axis=1, keepdims=True))
           - jnp.exp(jnp.sum(lv[2:3] * lv[3:4], axis=1, keepdims=True)) + lam_init)
    t = k_ref.shape[0]

    @pl.when(pl.program_id(1) == 0)
    def _():
        for h in range(A_HEADS):
            vx_ref[:, 2 * h * HEAD_COLS:(2 * h + 1) * HEAD_COLS] = v_ref[:, h * HEAD_COLS:(h + 1) * HEAD_COLS]
            vx_ref[:, (2 * h + 1) * HEAD_COLS:(2 * h + 2) * HEAD_COLS] = jnp.ones((t, HEAD_COLS), BF16)

    def run(n_keys):
        lane = lax.broadcasted_iota(jnp.int32, (TOKEN_TILE, HEAD_COLS), 1)
        zero = jnp.zeros((TOKEN_TILE, HEAD_COLS), BF16)

        def scores(chain):
            h, m = divmod(chain, 2)
            cols = slice(h * HEAD_COLS, (h + 1) * HEAD_COLS)
            qm = jnp.where((lane >= A_HEAD_DIM) if m else (lane < A_HEAD_DIM), q_ref[:, cols], zero)
            return lax.dot_general(qm, k_ref[0:n_keys, cols], (((1,), (1,)), ((), ())),
                                   preferred_element_type=F32)

        n_chains = 2 * A_HEADS
        res = []
        s_next = scores(0)
        for chain in range(n_chains):
            s = s_next
            if chain + 1 < n_chains:
                s_next = scores(chain + 1)
            p = jnp.exp2(s - jnp.max(s, axis=-1, keepdims=True)).astype(BF16)
            h = chain // 2
            res.append(_dot(p, vx_ref[0:n_keys, 2 * h * HEAD_COLS:(2 * h + 2) * HEAD_COLS]))
        outs = []
        for h in range(A_HEADS):
            r1, r2 = res[2 * h], res[2 * h + 1]
            o = (r1[:, :HEAD_COLS] / r1[:, HEAD_COLS:] - lam * (r2[:, :HEAD_COLS] / r2[:, HEAD_COLS:]))
            outs.append(_rms(o))
        y = jnp.concatenate(outs, axis=1) * g_ref[...] * (1.0 - lam_init)
        o_ref[...] = y.astype(BF16)

    if ctx_tile:
        pl.when(pl.program_id(1) == 0)(lambda: run(CTX_LEN))
        pl.when(pl.program_id(1) > 0)(lambda: run(t))
    else:
        run(t)


def _attention(u, lamvec, subln_g, lam_init, ctx_tile):
    b, t, _ = u.shape
    first = 0 if ctx_tile else 1
    nq = t // TOKEN_TILE - first
    tile = pl.BlockSpec((None, TOKEN_TILE, BRANCH_WIDTH), lambda bb, i: (bb, i, 0))
    return pl.pallas_call(
        functools.partial(_attn_kernel, lam_init=lam_init, ctx_tile=ctx_tile),
        grid=(b, nq),
        in_specs=[
            pl.BlockSpec((None, TOKEN_TILE, BRANCH_WIDTH), lambda bb, i: (bb, i + first, 0)),
            pl.BlockSpec((None, t, BRANCH_WIDTH), lambda bb, i: (bb, 0, 1)),
            pl.BlockSpec((None, t, BRANCH_WIDTH), lambda bb, i: (bb, 0, 2)),
            _const_spec((SUBLANES, LANES)),
            _const_spec((1, BRANCH_WIDTH)),
        ],
        out_specs=tile,
        out_shape=jax.ShapeDtypeStruct((b, nq * TOKEN_TILE, BRANCH_WIDTH), BF16),
        scratch_shapes=[pltpu.VMEM((t, 2 * BRANCH_WIDTH), BF16)],
        compiler_params=_params("parallel", "arbitrary"),
        name="diff_attention",
    )(u, u, u, lamvec, subln_g)


def _fourier_kernel(f_ref, ch_ref, sh_ref, rev_ref, wc_ref, csg_ref, o_ref, buf, *, ctx_tile):
    t = f_ref.shape[0]
    seq = t - CTX_LEN
    m = seq // 2
    out0 = CTX_LEN if ctx_tile else 0

    def channel_dft(rows):
        zc, zs = [], []
        for g in range(FOURIER_GROUPS):
            zz = _dot(f_ref[rows, g * FOURIER_GROUP:(g + 1) * FOURIER_GROUP], csg_ref[...])
            zc.append(zz[:, :FOURIER_GROUP])
            zs.append(zz[:, FOURIER_GROUP:])
        return jnp.concatenate(zc, axis=1), jnp.concatenate(zs, axis=1)

    def reversed_shifted(a, row0):
        ab = a.astype(BF16)
        nb = m // REV_BLOCK
        for c in range(nb):
            buf[SUBLANES + c * REV_BLOCK:SUBLANES + (c + 1) * REV_BLOCK, :] = _dot(
                rev_ref[...], ab[(nb - 1 - c) * REV_BLOCK:(nb - c) * REV_BLOCK, :])
        buf[SUBLANES - 1:SUBLANES, :] = row0
        return buf[SUBLANES - 1:SUBLANES - 1 + m, :]

    if ctx_tile:
        zc, zs = channel_dft(slice(0, CTX_LEN))
        o_ref[0:CTX_LEN, :] = (_dot(wc_ref[:, :CTX_LEN], zc.astype(BF16))
                               + _dot(wc_ref[:, CTX_LEN:], zs.astype(BF16))).astype(BF16)

    zc, zs = channel_dft(slice(CTX_LEN, t))
    zero_row = jnp.zeros((1, BRANCH_WIDTH), F32)
    e = zc[:m] + reversed_shifted(zc[m:], zero_row)
    o = zs[:m] - reversed_shifted(zs[m:], zero_row)
    row = lax.broadcasted_iota(jnp.int32, (m, BRANCH_WIDTH), 0)
    sign = jnp.where((row & 1) == 0, 1.0, -1.0)
    nyq = zc[m:m + 1] * seq ** -0.5
    p = _dot(ch_ref[...], e.astype(BF16)) + sign * nyq
    q = _dot(sh_ref[...], o.astype(BF16))
    o_ref[out0:out0 + m, :] = (p - q).astype(BF16)
    y_mid = jnp.sum(sign * e, axis=0, keepdims=True) * seq ** -0.5 + nyq
    o_ref[out0 + m:out0 + seq, :] = reversed_shifted(p + q, y_mid).astype(BF16)


def _fourier(u, ch, sh, rev, wc, csg, ctx_tile):
    b, t, _ = u.shape
    seq = t - CTX_LEN
    out_t = t if ctx_tile else seq
    return pl.pallas_call(
        functools.partial(_fourier_kernel, ctx_tile=ctx_tile),
        grid=(b,),
        in_specs=[
            pl.BlockSpec((None, t, BRANCH_WIDTH), lambda bb: (bb, 0, 6)),
            _const_spec((seq // 2, seq // 2)),
            _const_spec((seq // 2, seq // 2)),
            _const_spec((REV_BLOCK, REV_BLOCK)),
            _const_spec((CTX_LEN, 2 * CTX_LEN)),
            _const_spec((FOURIER_GROUP, 2 * FOURIER_GROUP)),
        ],
        out_specs=pl.BlockSpec((None, out_t, BRANCH_WIDTH), lambda bb: (bb, 0, 0)),
        out_shape=jax.ShapeDtypeStruct((b, out_t, BRANCH_WIDTH), BF16),
        scratch_shapes=[pltpu.VMEM((seq // 2 + 2 * SUBLANES, BRANCH_WIDTH), F32)],
        compiler_params=_params("parallel"),
        name="fourier",
    )(u, ch, sh, rev, wc, csg)


def _merge_kernel(*refs, n_stream, first, n_tiles):
    x_refs = refs[:n_stream]
    (h_ref, bg_ref, cg_ref, xs_ref, p_ref, cgp_ref, xsp_ref, pp_ref, cgn_ref, xsn_ref, pn_ref,
     ya_ref, yf_ref, mod_ref, convw_ref, convb_ref, wpool_ref, pscale_ref,
     wgate_ref, bgate_ref, wbr_ref, wout_ref, o_ref, zbuf, pbuf) = refs[n_stream:]
    i = pl.program_id(1) + first
    tm = TOKEN_TILE
    prev_ok = i >= 2
    next_ok = jnp.logical_and(i >= 1, i < n_tiles - 1)

    z = cg_ref[...].astype(F32) * xs_ref[...].astype(F32)
    zp = cgp_ref[...].astype(F32) * xsp_ref[...].astype(F32)
    zn = cgn_ref[...].astype(F32) * xsn_ref[...].astype(F32)
    zbuf[0:HALO, :] = jnp.where(prev_ok, zp, 0.0)
    zbuf[HALO:HALO + tm, :] = z
    zbuf[HALO + tm:, :] = jnp.where(next_ok, zn, 0.0)
    conv = (zbuf[HALO - 1:HALO - 1 + tm, :] * convw_ref[0:1, :] + z * convw_ref[1:2, :]
            + zbuf[HALO + 1:HALO + 1 + tm, :] * convw_ref[2:3, :] + convb_ref[...])
    y_conv = (bg_ref[...].astype(F32) * conv).astype(BF16)

    p = p_ref[...].astype(F32)
    pbuf[0:HALO, :] = jnp.where(prev_ok, pp_ref[...].astype(F32), 0.0)
    pbuf[HALO:HALO + tm, :] = p
    pbuf[HALO + tm:, :] = jnp.where(next_ok, pn_ref[...].astype(F32), 0.0)
    pos = (lax.broadcasted_iota(jnp.int32, (tm, POOL_GROUP), 0)
           + jnp.where(i == 0, 0, (i - 1) * tm))
    seq_len = jnp.where(i == 0, CTX_LEN, (n_tiles - 1) * tm)
    pool_parts = []
    for g, w in enumerate(POOL_WINDOWS):
        cols = slice(g * POOL_GROUP, (g + 1) * POOL_GROUP)
        acc = pbuf[HALO - w // 2:HALO - w // 2 + tm, cols]
        for d in range(1, w):
            acc = acc + pbuf[HALO - w // 2 + d:HALO - w // 2 + d + tm, cols]
        lo = jnp.maximum(pos - w // 2, 0)
        hi = jnp.minimum(pos - w // 2 + w, seq_len)
        pooled = (acc / (hi - lo).astype(F32) - p[:, cols]).astype(BF16)
        pool_parts.append(_dot(pooled, wpool_ref[g]) * pscale_ref[:, cols])
    y_pool = jnp.concatenate(pool_parts, axis=1).astype(BF16)

    ys = (ya_ref[...], y_conv, yf_ref[...], y_pool)
    hb = h_ref[...]
    acc = jnp.zeros((tm, D_MODEL), F32)
    for n in range(N_BRANCH):
        cols = slice(n * D_MODEL, (n + 1) * D_MODEL)
        gate = jax.nn.sigmoid(_dot(hb, wgate_ref[:, cols]) + bgate_ref[:, cols])
        acc = acc + gate * _dot(ys[n], wbr_ref[n])
    out = _dot(acc.astype(BF16), wout_ref[...])
    g1 = mod_ref[:, 2 * D_MODEL:3 * D_MODEL]
    o_ref[...] = _stream_tile(x_refs, i) + g1 * out


def _merge(streams, h, u, ya, yf, modsel, conv_w, conv_b, w_pool, pool_scale, w_gate, b_gate, w_br, w_out,
           layer, ctx_tile):
    b, t, _ = u.shape
    n_tiles = t // TOKEN_TILE
    first = 0 if ctx_tile else 1
    hb = TOKEN_TILE // HALO
    col = lambda j: pl.BlockSpec((None, TOKEN_TILE, BRANCH_WIDTH), lambda bb, i: (bb, i + first, j))
    prev = lambda j: pl.BlockSpec(
        (None, HALO, BRANCH_WIDTH), lambda bb, i: (bb, jnp.maximum((i + first) * hb - 1, 0), j))
    nxt = lambda j: pl.BlockSpec(
        (None, HALO, BRANCH_WIDTH), lambda bb, i: (bb, jnp.minimum((i + first + 1) * hb, t // HALO - 1), j))
    branch = pl.BlockSpec((None, TOKEN_TILE, BRANCH_WIDTH), lambda bb, i: (bb, i, 0))
    return pl.pallas_call(
        functools.partial(_merge_kernel, n_stream=len(streams), first=first, n_tiles=n_tiles),
        grid=(b, n_tiles - first),
        in_specs=_stream_specs(len(streams) == 2, D_MODEL, first) + [
            pl.BlockSpec((None, TOKEN_TILE, D_MODEL), lambda bb, i: (bb, i + first, 0)),
            col(3), col(4), col(5), col(7),
            prev(4), prev(5), prev(7), nxt(4), nxt(5), nxt(7),
            branch, branch,
            _mod_spec(first),
            _const_spec((CONV_WIDTH, BRANCH_WIDTH)),
            _const_spec((1, BRANCH_WIDTH)),
            _layer_spec((len(POOL_WINDOWS), POOL_GROUP, POOL_GROUP), layer),
            _const_spec((1, BRANCH_WIDTH)),
            _layer_spec((D_MODEL, N_BRANCH * D_MODEL), layer),
            _const_spec((1, N_BRANCH * D_MODEL)),
            _layer_spec((N_BRANCH, BRANCH_WIDTH, D_MODEL), layer),
            _layer_spec((D_MODEL, D_MODEL), layer),
        ],
        out_specs=pl.BlockSpec((None, TOKEN_TILE, D_MODEL), lambda bb, i: (bb, i, 0)),
        out_shape=jax.ShapeDtypeStruct((b, (n_tiles - first) * TOKEN_TILE, D_MODEL), F32),
        scratch_shapes=[
            pltpu.VMEM((TOKEN_TILE + 2 * HALO, BRANCH_WIDTH), F32),
            pltpu.VMEM((TOKEN_TILE + 2 * HALO, BRANCH_WIDTH), F32),
        ],
        compiler_params=_params("parallel", "arbitrary"),
        name="merge",
    )(*streams, h, u, u, u, u, u, u, u, u, u, u, ya, yf, modsel, conv_w, conv_b, w_pool, pool_scale,
      w_gate, b_gate, w_br, w_out)


def _ffn_kernel(x_ref, mod_ref, g_ref, w1_ref, b1_ref, w2_ref, b2_ref, fg_ref, o_ref, *, final):
    x = x_ref[...]
    shift = mod_ref[:, 3 * D_MODEL:4 * D_MODEL]
    scale = mod_ref[:, 4 * D_MODEL:5 * D_MODEL]
    gate = mod_ref[:, 5 * D_MODEL:6 * D_MODEL]
    hb = (_rms(x) * g_ref[...] * (1.0 + scale) + shift).astype(BF16)
    acc = jnp.zeros(x.shape, F32)
    for c in range(D_FF // D_MODEL):
        cols = slice(c * D_MODEL, (c + 1) * D_MODEL)
        hid = jnp.square(jnp.maximum(_dot(hb, w1_ref[:, cols]) + b1_ref[:, cols], 0.0))
        acc = acc + _dot(hid.astype(BF16), w2_ref[cols, :])
    y = x + gate * (acc + b2_ref[...])
    if final:
        y = _rms(y) * fg_ref[...]
    o_ref[...] = y


def _ffn(xs, modsel, gain, w1, b1, w2, b2, final_g, layer, latent_only, final):
    b, rows, _ = xs.shape
    tile = pl.BlockSpec((None, TOKEN_TILE, D_MODEL), lambda bb, i: (bb, i, 0))
    return pl.pallas_call(
        functools.partial(_ffn_kernel, final=final),
        grid=(b, rows // TOKEN_TILE),
        in_specs=[
            tile,
            _mod_spec(1 if latent_only else 0),
            _const_spec((1, D_MODEL)),
            _layer_spec((D_MODEL, D_FF), layer),
            _const_spec((1, D_FF)),
            _layer_spec((D_FF, D_MODEL), layer),
            _const_spec((1, D_MODEL)),
            _const_spec((1, D_MODEL)),
        ],
        out_specs=tile,
        out_shape=jax.ShapeDtypeStruct(xs.shape, F32),
        compiler_params=_params("parallel", "arbitrary"),
        name="ffn",
    )(xs, modsel, gain, w1, b1, w2, b2, final_g)


def _rope_tables(seq):
    lane = jnp.arange(LANES, dtype=jnp.int32)
    d = lane % A_HEAD_DIM
    axis = d // AXIS_ROT
    upper = (d % AXIS_ROT) // (AXIS_ROT // 2)
    freq = (d % (AXIS_ROT // 2)).astype(F32)
    inv = ROPE_BASE ** (-freq * 2.0 / AXIS_ROT)
    tok = jnp.arange(seq, dtype=jnp.int32)
    pos = jnp.where(axis[None, :] == 0, (tok // GRID_W)[:, None], (tok % GRID_W)[:, None]).astype(F32)
    ang = pos * inv[None, :]
    cos, sin = jnp.cos(ang), jnp.sin(ang)
    s_up = jnp.where(upper[None, :] == 0, -sin, 0.0)
    s_dn = jnp.where(upper[None, :] == 1, sin, 0.0)
    pad = lambda a, v: jnp.concatenate([jnp.full((CTX_LEN, LANES), v, F32), a], axis=0)
    return pad(cos, 1.0), pad(s_up, 0.0), pad(s_dn, 0.0)


def _dft_cos_sin(n, scale):
    k = jnp.arange(n, dtype=jnp.int32)
    ang = ((k[:, None] * k[None, :]) % n).astype(F32) * (2.0 * math.pi / n)
    return jnp.cos(ang) * scale, jnp.sin(ang) * scale


def _dft_corner(n):
    half, lo = n // 2, 64
    t = jnp.arange(half, dtype=jnp.int32)

    def tables(k):
        ang = ((k[:, None] * t[None, :]) % n).astype(F32) * (2.0 * math.pi / n)
        return jnp.cos(ang), jnp.sin(ang)

    ca, sa = tables(jnp.arange(half // lo, dtype=jnp.int32) * lo)
    cb, sb = tables(jnp.arange(lo, dtype=jnp.int32))
    cos = (ca[:, None, :] * cb[None, :, :] - sa[:, None, :] * sb[None, :, :]).reshape(half, half)
    sin = (sa[:, None, :] * cb[None, :, :] + ca[:, None, :] * sb[None, :, :]).reshape(half, half)
    return (cos * n ** -0.5).astype(BF16), (sin * n ** -0.5).astype(BF16)


def kernel(x, c, ctx, c_ctx, w_mod, b_mod, norm1_g, w_in, lam_q1, lam_k1, lam_q2, lam_k2, subln_g, conv_w,
           conv_b, w_pool, pool_scale, w_gate, b_gate, w_br, w_out, norm2_g, w_ff1, b_ff1, w_ff2, b_ff2,
           final_g):
    batch, seq, d_model = x.shape
    depth = w_mod.shape[0]
    assert d_model == D_MODEL and ctx.shape[1] == CTX_LEN == TOKEN_TILE
    assert seq % (2 * REV_BLOCK) == 0 and batch + 1 <= COND_ROWS

    cond = jnp.concatenate([c, c_ctx[None, :], jnp.zeros((COND_ROWS - batch - 1, D_MODEL), F32)], axis=0)
    mods = _adaln(cond, w_mod, b_mod)

    rope = _rope_tables(seq)
    ch, sh = _dft_corner(seq)
    cc, sc = _dft_cos_sin(CTX_LEN, CTX_LEN ** -0.5)
    wc = jnp.concatenate([cc, -sc], axis=1).astype(BF16)
    cg, sg = _dft_cos_sin(FOURIER_GROUP, FOURIER_GROUP ** -0.5)
    csg = jnp.concatenate([cg, sg], axis=1).astype(BF16)
    rev = jnp.flip(jnp.eye(REV_BLOCK, dtype=BF16), axis=0)

    w_in, w_pool, w_gate, w_br, w_out, w_ff1, w_ff2 = (
        w.astype(BF16) for w in (w_in, w_pool, w_gate, w_br, w_out, w_ff1, w_ff2))

    streams = (ctx, x)
    row = lambda a: a.reshape(1, -1)
    for l in range(depth):
        last = l == depth - 1
        lam_init = 0.8 - 0.6 * math.exp(-0.3 * l)
        m = mods[l]
        modsel = jnp.stack([jnp.broadcast_to(m[batch], (batch, N_MOD * D_MODEL)), m[:batch]],
                           axis=1)[:, :, None, :]
        lamvec = jnp.pad(jnp.stack([lam_q1[l], lam_k1[l], lam_q2[l], lam_k2[l]]),
                         ((0, SUBLANES - 4), (0, LANES - A_HEAD_DIM)))

        u, h = _inproj(streams, modsel, row(norm1_g[l]), w_in, l, rope)
        ya = _attention(u, lamvec, row(jnp.tile(subln_g[l], A_HEADS)), lam_init, ctx_tile=not last)
        yf = _fourier(u, ch, sh, rev, wc, csg, ctx_tile=not last)
        xs = _merge(streams, h, u, ya, yf, modsel, conv_w[l], row(conv_b[l]), w_pool, row(pool_scale[l]),
                    w_gate, row(b_gate[l]), w_br, w_out, l, ctx_tile=not last)
        xs = _ffn(xs, modsel, row(norm2_g[l]), w_ff1, row(b_ff1[l]), w_ff2, row(b_ff2[l]), row(final_g),
                  l, latent_only=last, final=last)
        streams = (xs,)
    return xs
```

```python
import functools
import math

import jax
import jax.numpy as jnp
import numpy as np
from jax import lax
from jax.experimental import pallas as pl
from jax.experimental.pallas import tpu as pltpu

F32 = jnp.float32
BF16 = jnp.bfloat16

D_MODEL = 1024
CTX_LEN = 256
GRID_W = 64
N_BRANCH = 4
BRANCH_WIDTH = 512
A_HEADS = 4
A_HEAD_DIM = 64
HEAD_COLS = 2 * A_HEAD_DIM
AXIS_ROT = A_HEAD_DIM // 2
ROPE_BASE = 10000.0
ATTN_SCALE = A_HEAD_DIM ** -0.5
CONV_WIDTH = 3
FOURIER_GROUPS = 4
FOURIER_GROUP = BRANCH_WIDTH // FOURIER_GROUPS
POOL_WINDOWS = (2, 4, 8, 16)
POOL_GROUP = BRANCH_WIDTH // len(POOL_WINDOWS)
N_MOD = 6
NORM_EPS = 1e-6
IN_WIDTH = 8 * BRANCH_WIDTH
D_FF = 4 * D_MODEL

TOKEN_TILE = 256
PAIR = 2
HALO = 16
COND_ROWS = 16
LANES = 128
SUBLANES = 8
REV_BLOCK = 256
LOG2_E = math.log2(math.e)
VMEM_LIMIT = 52 * 1024 * 1024


def _dot(a, b):
    return jnp.dot(a, b, preferred_element_type=F32)


def _rms(x):
    return x * lax.rsqrt(jnp.mean(x * x, axis=-1, keepdims=True) + NORM_EPS)


def _params(*sem):
    return pltpu.CompilerParams(dimension_semantics=sem, vmem_limit_bytes=VMEM_LIMIT)


def _const_spec(shape):
    zeros = (0,) * len(shape)
    return pl.BlockSpec(shape, lambda *_: zeros)


def _layer_spec(shape, layer):
    zeros = (0,) * len(shape)
    return pl.BlockSpec((None,) + shape, lambda *_: (layer,) + zeros)


def _mod_spec(first):
    return pl.BlockSpec((None, None, 1, N_MOD * D_MODEL), lambda bb, i: (bb, jnp.minimum(i + first, 1), 0, 0))


def _stream_specs(split, width, first):
    if split:
        return [pl.BlockSpec((None, TOKEN_TILE, width), lambda bb, i: (bb, 0, 0)),
                pl.BlockSpec((None, TOKEN_TILE, width), lambda bb, i: (bb, jnp.maximum(i + first - 1, 0), 0))]
    return [pl.BlockSpec((None, TOKEN_TILE, width), lambda bb, i: (bb, i + first, 0))]


def _stream_tile(refs, i):
    if len(refs) == 2:
        return jnp.where(i == 0, refs[0][...], refs[1][...])
    return refs[0][...]


def _adaln_kernel(cond_ref, w_ref, b_ref, o_ref):
    c = cond_ref[...]
    s = c * jax.nn.sigmoid(c)
    o_ref[...] = _dot(s.astype(BF16), w_ref[...].astype(BF16)) + b_ref[...]


def _adaln(cond, w_mod, b_mod):
    depth = w_mod.shape[0]
    return pl.pallas_call(
        _adaln_kernel,
        grid=(depth, N_MOD),
        in_specs=[
            pl.BlockSpec((COND_ROWS, D_MODEL), lambda l, j: (0, 0)),
            pl.BlockSpec((None, D_MODEL, D_MODEL), lambda l, j: (l, 0, j)),
            pl.BlockSpec((None, 1, D_MODEL), lambda l, j: (l, 0, j)),
        ],
        out_specs=pl.BlockSpec((None, COND_ROWS, D_MODEL), lambda l, j: (l, 0, j)),
        out_shape=jax.ShapeDtypeStruct((depth, COND_ROWS, N_MOD * D_MODEL), F32),
        compiler_params=_params("arbitrary", "arbitrary"),
        name="adaln",
    )(cond, w_mod, b_mod.reshape(depth, 1, N_MOD * D_MODEL))


def _inproj_body(x_tiles, mod_refs, rope_refs, g_ref, w_ref, u_ref, h_ref):
    hs = []
    for x, mod_ref in zip(x_tiles, mod_refs):
        shift = mod_ref[:, 0:D_MODEL]
        scale = mod_ref[:, D_MODEL:2 * D_MODEL]
        hs.append((_rms(x) * g_ref[...] * (1.0 + scale) + shift).astype(BF16))
    hb = jnp.concatenate(hs, axis=0)
    h_ref[...] = hb
    for j in range(IN_WIDTH // BRANCH_WIDTH):
        cols = slice(j * BRANCH_WIDTH, (j + 1) * BRANCH_WIDTH)
        u = _dot(hb, w_ref[:, cols])
        if j < 2:
            tiles = []
            for k, (cos_ref, sup_ref, sdn_ref) in enumerate(rope_refs):
                parts = []
                for c in range(BRANCH_WIDTH // LANES):
                    uc = u[k * TOKEN_TILE:(k + 1) * TOKEN_TILE, c * LANES:(c + 1) * LANES]
                    up = pltpu.roll(uc, LANES - AXIS_ROT // 2, axis=1)
                    dn = pltpu.roll(uc, AXIS_ROT // 2, axis=1)
                    parts.append(uc * cos_ref[...] + up * sup_ref[...] + dn * sdn_ref[...])
                tiles.append(jnp.concatenate(parts, axis=1))
            u = jnp.concatenate(tiles, axis=0)
            if j == 0:
                u = u * (ATTN_SCALE * LOG2_E)
        u_ref[:, cols] = u.astype(BF16)


def _inproj_split_kernel(ctx_ref, x_ref, mod_ref, g_ref, w_ref, cos_ref, sup_ref, sdn_ref, u_ref, h_ref):
    x = _stream_tile((ctx_ref, x_ref), pl.program_id(1))
    _inproj_body([x], [mod_ref], [(cos_ref, sup_ref, sdn_ref)], g_ref, w_ref, u_ref, h_ref)


def _inproj_pair_kernel(*refs):
    x_ref = refs[0]
    mod_refs = refs[1:1 + PAIR]
    g_ref, w_ref = refs[1 + PAIR:3 + PAIR]
    rope_refs = [refs[3 + PAIR + 3 * k:6 + PAIR + 3 * k] for k in range(PAIR)]
    u_ref, h_ref = refs[3 + 4 * PAIR:]
    x_tiles = [x_ref[k * TOKEN_TILE:(k + 1) * TOKEN_TILE, :] for k in range(PAIR)]
    _inproj_body(x_tiles, mod_refs, rope_refs, g_ref, w_ref, u_ref, h_ref)


def _inproj_split(ctx, x, modsel, gain, w_in, layer, rope):
    b = x.shape[0]
    t = rope[0].shape[0]
    tile = lambda bb, i: (bb, i, 0)
    rope_spec = pl.BlockSpec((TOKEN_TILE, LANES), lambda bb, i: (i, 0))
    return pl.pallas_call(
        _inproj_split_kernel,
        grid=(b, t // TOKEN_TILE),
        in_specs=_stream_specs(True, D_MODEL, 0) + [
            _mod_spec(0),
            _const_spec((1, D_MODEL)),
            _layer_spec((D_MODEL, IN_WIDTH), layer),
            rope_spec, rope_spec, rope_spec,
        ],
        out_specs=[
            pl.BlockSpec((None, TOKEN_TILE, IN_WIDTH), tile),
            pl.BlockSpec((None, TOKEN_TILE, D_MODEL), tile),
        ],
        out_shape=[
            jax.ShapeDtypeStruct((b, t, IN_WIDTH), BF16),
            jax.ShapeDtypeStruct((b, t, D_MODEL), BF16),
        ],
        compiler_params=_params("parallel", "arbitrary"),
        name="inproj",
    )(ctx, x, modsel, gain, w_in, *rope)


def _inproj_pair(xs, modsel, gain, w_in, layer, rope):
    b, t, _ = xs.shape
    nt = t // TOKEN_TILE
    assert (b * nt) % PAIR == 0

    def mod(k):
        def index(s):
            tile = s * PAIR + k
            return (tile // nt, jnp.minimum(tile % nt, 1), 0, 0)
        return pl.BlockSpec((None, None, 1, N_MOD * D_MODEL), index)

    def rope_specs(k):
        return [pl.BlockSpec((TOKEN_TILE, LANES), lambda s: ((s * PAIR + k) % nt, 0))] * 3

    rows = lambda width: pl.BlockSpec((PAIR * TOKEN_TILE, width), lambda s: (s, 0))
    u, h = pl.pallas_call(
        _inproj_pair_kernel,
        grid=(b * nt // PAIR,),
        in_specs=[rows(D_MODEL)] + [mod(k) for k in range(PAIR)] + [
            _const_spec((1, D_MODEL)),
            _layer_spec((D_MODEL, IN_WIDTH), layer),
        ] + [spec for k in range(PAIR) for spec in rope_specs(k)],
        out_specs=[rows(IN_WIDTH), rows(D_MODEL)],
        out_shape=[
            jax.ShapeDtypeStruct((b * t, IN_WIDTH), BF16),
            jax.ShapeDtypeStruct((b * t, D_MODEL), BF16),
        ],
        compiler_params=_params("arbitrary"),
        name="inproj",
    )(xs.reshape(b * t, D_MODEL), *([modsel] * PAIR), gain, w_in, *(rope * PAIR))
    return u.reshape(b, t, IN_WIDTH), h.reshape(b, t, D_MODEL)


def _attn_kernel(q_ref, k_ref, v_ref, lam_ref, g_ref, o_ref, vx_ref, *, lam_init, ctx_tile):
    lv = lam_ref[...]
    lam = (jnp.exp(jnp.sum(lv[0:1] * lv[1:2], axis=1, keepdims=True))
           - jnp.exp(jnp.sum(lv[2:3] * lv[3:4], axis=1, keepdims=True)) + lam_init)
    t = k_ref.shape[0]

    @pl.when(pl.program_id(1) == 0)
    def _():
        for h in range(A_HEADS):
            vx_ref[:, 2 * h * HEAD_COLS:(2 * h + 1) * HEAD_COLS] = v_ref[:, h * HEAD_COLS:(h + 1) * HEAD_COLS]
            vx_ref[:, (2 * h + 1) * HEAD_COLS:(2 * h + 2) * HEAD_COLS] = jnp.ones((t, HEAD_COLS), BF16)

    def run(n_keys):
        lane = lax.broadcasted_iota(jnp.int32, (TOKEN_TILE, HEAD_COLS), 1)
        zero = jnp.zeros((TOKEN_TILE, HEAD_COLS), BF16)

        def scores(chain):
            h, m = divmod(chain, 2)
            cols = slice(h * HEAD_COLS, (h + 1) * HEAD_COLS)
            qm = jnp.where((lane >= A_HEAD_DIM) if m else (lane < A_HEAD_DIM), q_ref[:, cols], zero)
            return lax.dot_general(qm, k_ref[0:n_keys, cols], (((1,), (1,)), ((), ())),
                                   preferred_element_type=F32)

        n_chains = 2 * A_HEADS
        res = []
        s_next = scores(0)
        for chain in range(n_chains):
            s = s_next
            if chain + 1 < n_chains:
                s_next = scores(chain + 1)
            p = jnp.exp2(s - jnp.max(s, axis=-1, keepdims=True)).astype(BF16)
            h = chain // 2
            res.append(_dot(p, vx_ref[0:n_keys, 2 * h * HEAD_COLS:(2 * h + 2) * HEAD_COLS]))
        outs = []
        for h in range(A_HEADS):
            r1, r2 = res[2 * h], res[2 * h + 1]
            o = (r1[:, :HEAD_COLS] / r1[:, HEAD_COLS:] - lam * (r2[:, :HEAD_COLS] / r2[:, HEAD_COLS:]))
            outs.append(_rms(o))
        y = jnp.concatenate(outs, axis=1) * g_ref[...] * (1.0 - lam_init)
        o_ref[...] = y.astype(BF16)

    if ctx_tile:
        pl.when(pl.program_id(1) == 0)(lambda: run(CTX_LEN))
        pl.when(pl.program_id(1) > 0)(lambda: run(t))
    else:
        run(t)


def _attention(u, lamvec, subln_g, lam_init, ctx_tile):
    b, t, _ = u.shape
    first = 0 if ctx_tile else 1
    nq = t // TOKEN_TILE - first
    tile = pl.BlockSpec((None, TOKEN_TILE, BRANCH_WIDTH), lambda bb, i: (bb, i, 0))
    return pl.pallas_call(
        functools.partial(_attn_kernel, lam_init=lam_init, ctx_tile=ctx_tile),
        grid=(b, nq),
        in_specs=[
            pl.BlockSpec((None, TOKEN_TILE, BRANCH_WIDTH), lambda bb, i: (bb, i + first, 0)),
            pl.BlockSpec((None, t, BRANCH_WIDTH), lambda bb, i: (bb, 0, 1)),
            pl.BlockSpec((None, t, BRANCH_WIDTH), lambda bb, i: (bb, 0, 2)),
            _const_spec((SUBLANES, LANES)),
            _const_spec((1, BRANCH_WIDTH)),
        ],
        out_specs=tile,
        out_shape=jax.ShapeDtypeStruct((b, nq * TOKEN_TILE, BRANCH_WIDTH), BF16),
        scratch_shapes=[pltpu.VMEM((t, 2 * BRANCH_WIDTH), BF16)],
        compiler_params=_params("parallel", "arbitrary"),
        name="diff_attention",
    )(u, u, u, lamvec, subln_g)


def _fourier_kernel(f_ref, ch_ref, sh_ref, rev_ref, wc_ref, csg_ref, o_ref, buf, *, ctx_tile):
    t = f_ref.shape[0]
    seq = t - CTX_LEN
    m = seq // 2
    out0 = CTX_LEN if ctx_tile else 0

    def channel_dft(rows):
        zc, zs = [], []
        for g in range(FOURIER_GROUPS):
            zz = _dot(f_ref[rows, g * FOURIER_GROUP:(g + 1) * FOURIER_GROUP], csg_ref[...])
            zc.append(zz[:, :FOURIER_GROUP])
            zs.append(zz[:, FOURIER_GROUP:])
        return jnp.concatenate(zc, axis=1), jnp.concatenate(zs, axis=1)

    def reversed_shifted(a, row0):
        ab = a.astype(BF16)
        nb = m // REV_BLOCK
        for c in range(nb):
            buf[SUBLANES + c * REV_BLOCK:SUBLANES + (c + 1) * REV_BLOCK, :] = _dot(
                rev_ref[...], ab[(nb - 1 - c) * REV_BLOCK:(nb - c) * REV_BLOCK, :])
        buf[SUBLANES - 1:SUBLANES, :] = row0
        return buf[SUBLANES - 1:SUBLANES - 1 + m, :]

    if ctx_tile:
        zc, zs = channel_dft(slice(0, CTX_LEN))
        o_ref[0:CTX_LEN, :] = (_dot(wc_ref[:, :CTX_LEN], zc.astype(BF16))
                               + _dot(wc_ref[:, CTX_LEN:], zs.astype(BF16))).astype(BF16)

    zc, zs = channel_dft(slice(CTX_LEN, t))
    zero_row = jnp.zeros((1, BRANCH_WIDTH), F32)
    e = zc[:m] + reversed_shifted(zc[m:], zero_row)
    o = zs[:m] - reversed_shifted(zs[m:], zero_row)
    row = lax.broadcasted_iota(jnp.int32, (m, BRANCH_WIDTH), 0)
    sign = jnp.where((row & 1) == 0, 1.0, -1.0)
    nyq = zc[m:m + 1] * seq ** -0.5
    p = _dot(ch_ref[...], e.astype(BF16)) + sign * nyq
    q = _dot(sh_ref[...], o.astype(BF16))
    o_ref[out0:out0 + m, :] = (p - q).astype(BF16)
    y_mid = jnp.sum(sign * e, axis=0, keepdims=True) * seq ** -0.5 + nyq
    o_ref[out0 + m:out0 + seq, :] = reversed_shifted(p + q, y_mid).astype(BF16)


def _fourier(u, ch, sh, rev, wc, csg, ctx_tile):
    b, t, _ = u.shape
    seq = t - CTX_LEN
    out_t = t if ctx_tile else seq
    return pl.pallas_call(
        functools.partial(_fourier_kernel, ctx_tile=ctx_tile),
        grid=(b,),
        in_specs=[
            pl.BlockSpec((None, t, BRANCH_WIDTH), lambda bb: (bb, 0, 6)),
            _const_spec((seq // 2, seq // 2)),
            _const_spec((seq // 2, seq // 2)),
            _const_spec((REV_BLOCK, REV_BLOCK)),
            _const_spec((CTX_LEN, 2 * CTX_LEN)),
            _const_spec((FOURIER_GROUP, 2 * FOURIER_GROUP)),
        ],
        out_specs=pl.BlockSpec((None, out_t, BRANCH_WIDTH), lambda bb: (bb, 0, 0)),
        out_shape=jax.ShapeDtypeStruct((b, out_t, BRANCH_WIDTH), BF16),
        scratch_shapes=[pltpu.VMEM((seq // 2 + 2 * SUBLANES, BRANCH_WIDTH), F32)],
        compiler_params=_params("parallel"),
        name="fourier",
    )(u, ch, sh, rev, wc, csg)


def _merge_kernel(*refs, n_stream, first, n_tiles):
    x_refs = refs[:n_stream]
    (h_ref, bg_ref, cg_ref, xs_ref, p_ref, cgp_ref, xsp_ref, pp_ref, cgn_ref, xsn_ref, pn_ref,
     ya_ref, yf_ref, mod_ref, convw_ref, convb_ref, band_ref, wpool_ref, pscale_ref,
     wgate_ref, bgate_ref, wbr_ref, wout_ref, o_ref, zbuf) = refs[n_stream:]
    i = pl.program_id(1) + first
    tm = TOKEN_TILE
    prev_ok = i >= 2
    next_ok = jnp.logical_and(i >= 1, i < n_tiles - 1)

    hb = h_ref[...]

    def gate_logits(n):
        return _dot(hb, wgate_ref[:, n * D_MODEL:(n + 1) * D_MODEL])

    def gated(logits, n, proj):
        return jax.nn.sigmoid(logits + bgate_ref[:, n * D_MODEL:(n + 1) * D_MODEL]) * proj

    logits = [gate_logits(n) for n in (0, 2, 1, 3)]
    proj_a = _dot(ya_ref[...], wbr_ref[0])
    proj_f = _dot(yf_ref[...], wbr_ref[2])

    z = cg_ref[...].astype(F32) * xs_ref[...].astype(F32)
    zp = cgp_ref[...].astype(F32) * xsp_ref[...].astype(F32)
    zn = cgn_ref[...].astype(F32) * xsn_ref[...].astype(F32)
    zbuf[0:HALO, :] = jnp.where(prev_ok, zp, 0.0)
    zbuf[HALO:HALO + tm, :] = z
    zbuf[HALO + tm:, :] = jnp.where(next_ok, zn, 0.0)
    conv = (zbuf[HALO - 1:HALO - 1 + tm, :] * convw_ref[0:1, :] + z * convw_ref[1:2, :]
            + zbuf[HALO + 1:HALO + 1 + tm, :] * convw_ref[2:3, :] + convb_ref[...])
    y_conv = (bg_ref[...].astype(F32) * conv).astype(BF16)

    pb = p_ref[...]
    p = pb.astype(F32)
    halo_zero = jnp.zeros((HALO, BRANCH_WIDTH), BF16)
    p_ext = jnp.concatenate([jnp.where(prev_ok, pp_ref[...], halo_zero), pb,
                             jnp.where(next_ok, pn_ref[...], halo_zero)], axis=0)
    pos = (lax.broadcasted_iota(jnp.int32, (tm, POOL_GROUP), 0)
           + jnp.where(i == 0, 0, (i - 1) * tm))
    seq_len = jnp.where(i == 0, CTX_LEN, (n_tiles - 1) * tm)
    pool_parts = []
    for g, w in enumerate(POOL_WINDOWS):
        cols = slice(g * POOL_GROUP, (g + 1) * POOL_GROUP)
        win = _dot(band_ref[g], p_ext[:, cols])
        lo = jnp.maximum(pos - w // 2, 0)
        hi = jnp.minimum(pos - w // 2 + w, seq_len)
        pooled = (win / (hi - lo).astype(F32) - p[:, cols]).astype(BF16)
        pool_parts.append(_dot(pooled, wpool_ref[g]) * pscale_ref[:, cols])
    y_pool = jnp.concatenate(pool_parts, axis=1).astype(BF16)

    acc = gated(logits[0], 0, proj_a) + gated(logits[1], 2, proj_f)
    acc = acc + gated(logits[2], 1, _dot(y_conv, wbr_ref[1]))
    acc = acc + gated(logits[3], 3, _dot(y_pool, wbr_ref[3]))
    out = _dot(acc.astype(BF16), wout_ref[...])
    g1 = mod_ref[:, 2 * D_MODEL:3 * D_MODEL]
    o_ref[...] = _stream_tile(x_refs, i) + g1 * out


def _merge(streams, h, u, ya, yf, modsel, conv_w, conv_b, w_pool, pool_scale, w_gate, b_gate, w_br, w_out,
           layer, ctx_tile):
    b, t, _ = u.shape
    n_tiles = t // TOKEN_TILE
    first = 0 if ctx_tile else 1
    hb = TOKEN_TILE // HALO
    col = lambda j: pl.BlockSpec((None, TOKEN_TILE, BRANCH_WIDTH), lambda bb, i: (bb, i + first, j))
    prev = lambda j: pl.BlockSpec(
        (None, HALO, BRANCH_WIDTH), lambda bb, i: (bb, jnp.maximum((i + first) * hb - 1, 0), j))
    nxt = lambda j: pl.BlockSpec(
        (None, HALO, BRANCH_WIDTH), lambda bb, i: (bb, jnp.minimum((i + first + 1) * hb, t // HALO - 1), j))
    branch = pl.BlockSpec((None, TOKEN_TILE, BRANCH_WIDTH), lambda bb, i: (bb, i, 0))
    return pl.pallas_call(
        functools.partial(_merge_kernel, n_stream=len(streams), first=first, n_tiles=n_tiles),
        grid=(b, n_tiles - first),
        in_specs=_stream_specs(len(streams) == 2, D_MODEL, first) + [
            pl.BlockSpec((None, TOKEN_TILE, D_MODEL), lambda bb, i: (bb, i + first, 0)),
            col(3), col(4), col(5), col(7),
            prev(4), prev(5), prev(7), nxt(4), nxt(5), nxt(7),
            branch, branch,
            _mod_spec(first),
            _const_spec((CONV_WIDTH, BRANCH_WIDTH)),
            _const_spec((1, BRANCH_WIDTH)),
            _const_spec((len(POOL_WINDOWS), TOKEN_TILE, TOKEN_TILE + 2 * HALO)),
            _layer_spec((len(POOL_WINDOWS), POOL_GROUP, POOL_GROUP), layer),
            _const_spec((1, BRANCH_WIDTH)),
            _layer_spec((D_MODEL, N_BRANCH * D_MODEL), layer),
            _const_spec((1, N_BRANCH * D_MODEL)),
            _layer_spec((N_BRANCH, BRANCH_WIDTH, D_MODEL), layer),
            _layer_spec((D_MODEL, D_MODEL), layer),
        ],
        out_specs=pl.BlockSpec((None, TOKEN_TILE, D_MODEL), lambda bb, i: (bb, i, 0)),
        out_shape=jax.ShapeDtypeStruct((b, (n_tiles - first) * TOKEN_TILE, D_MODEL), F32),
        scratch_shapes=[pltpu.VMEM((TOKEN_TILE + 2 * HALO, BRANCH_WIDTH), F32)],
        compiler_params=_params("parallel", "arbitrary"),
        name="merge",
    )(*streams, h, u, u, u, u, u, u, u, u, u, u, ya, yf, modsel, conv_w, conv_b, _pool_bands(), w_pool,
      pool_scale, w_gate, b_gate, w_br, w_out)


def _ffn_kernel(*refs, final):
    x_ref = refs[0]
    mod_refs = refs[1:1 + PAIR]
    g_ref, w1_ref, b1_ref, w2_ref, b2_ref, fg_ref, o_ref = refs[1 + PAIR:]
    tiles = [(slice(k * TOKEN_TILE, (k + 1) * TOKEN_TILE), mod_refs[k]) for k in range(PAIR)]

    def normed(rows, mod_ref):
        shift = mod_ref[:, 3 * D_MODEL:4 * D_MODEL]
        scale = mod_ref[:, 4 * D_MODEL:5 * D_MODEL]
        return (_rms(x_ref[rows, :]) * g_ref[...] * (1.0 + scale) + shift).astype(BF16)

    hb = jnp.concatenate([normed(rows, mod_ref) for rows, mod_ref in tiles], axis=0)
    acc = jnp.zeros(x_ref.shape, F32)
    for c in range(D_FF // D_MODEL):
        cols = slice(c * D_MODEL, (c + 1) * D_MODEL)
        hid = jnp.square(jnp.maximum(_dot(hb, w1_ref[:, cols]) + b1_ref[:, cols], 0.0))
        acc = acc + _dot(hid.astype(BF16), w2_ref[cols, :])
    for rows, mod_ref in tiles:
        gate = mod_ref[:, 5 * D_MODEL:6 * D_MODEL]
        y = x_ref[rows, :] + gate * (acc[rows] + b2_ref[...])
        if final:
            y = _rms(y) * fg_ref[...]
        o_ref[rows, :] = y


def _ffn(xs, modsel, gain, w1, b1, w2, b2, final_g, layer, latent_only, final):
    b, rows, _ = xs.shape
    nt = rows // TOKEN_TILE
    first = 1 if latent_only else 0
    assert (b * nt) % PAIR == 0

    def mod(k):
        def index(s):
            tile = s * PAIR + k
            return (tile // nt, jnp.minimum(tile % nt + first, 1), 0, 0)
        return pl.BlockSpec((None, None, 1, N_MOD * D_MODEL), index)

    block = pl.BlockSpec((PAIR * TOKEN_TILE, D_MODEL), lambda s: (s, 0))
    out = pl.pallas_call(
        functools.partial(_ffn_kernel, final=final),
        grid=(b * nt // PAIR,),
        in_specs=[block] + [mod(k) for k in range(PAIR)] + [
            _const_spec((1, D_MODEL)),
            _layer_spec((D_MODEL, D_FF), layer),
            _const_spec((1, D_FF)),
            _layer_spec((D_FF, D_MODEL), layer),
            _const_spec((1, D_MODEL)),
            _const_spec((1, D_MODEL)),
        ],
        out_specs=block,
        out_shape=jax.ShapeDtypeStruct((b * rows, D_MODEL), F32),
        compiler_params=_params("arbitrary"),
        name="ffn",
    )(xs.reshape(b * rows, D_MODEL), *([modsel] * PAIR), gain, w1, b1, w2, b2, final_g)
    return out.reshape(b, rows, D_MODEL)


def _rope_tables(seq):
    lane = np.arange(LANES)
    d = lane % A_HEAD_DIM
    axis = d // AXIS_ROT
    upper = (d % AXIS_ROT) // (AXIS_ROT // 2)
    inv = ROPE_BASE ** (-(d % (AXIS_ROT // 2)) * 2.0 / AXIS_ROT)
    tok = np.arange(seq)
    pos = np.where(axis[None, :] == 0, (tok // GRID_W)[:, None], (tok % GRID_W)[:, None])
    ang = pos * inv[None, :]
    cos, sin = np.cos(ang), np.sin(ang)
    s_up = np.where(upper[None, :] == 0, -sin, 0.0)
    s_dn = np.where(upper[None, :] == 1, sin, 0.0)
    pad = lambda a, v: jnp.asarray(np.concatenate([np.full((CTX_LEN, LANES), v), a], axis=0), F32)
    return pad(cos, 1.0), pad(s_up, 0.0), pad(s_dn, 0.0)


def _pool_bands():
    tok = np.arange(TOKEN_TILE)[:, None]
    src = np.arange(TOKEN_TILE + 2 * HALO)[None, :] - HALO
    bands = np.stack([(src >= tok - w // 2) & (src < tok - w // 2 + w) for w in POOL_WINDOWS])
    return jnp.asarray(bands, F32).astype(BF16)


def _dft_cos_sin(n, rows, cols):
    ang = (np.arange(rows)[:, None] * np.arange(cols)[None, :] % n) * (2.0 * math.pi / n)
    return jnp.asarray(np.cos(ang) * n ** -0.5, F32), jnp.asarray(np.sin(ang) * n ** -0.5, F32)


def kernel(x, c, ctx, c_ctx, w_mod, b_mod, norm1_g, w_in, lam_q1, lam_k1, lam_q2, lam_k2, subln_g, conv_w,
           conv_b, w_pool, pool_scale, w_gate, b_gate, w_br, w_out, norm2_g, w_ff1, b_ff1, w_ff2, b_ff2,
           final_g):
    batch, seq, d_model = x.shape
    depth = w_mod.shape[0]
    assert d_model == D_MODEL and ctx.shape[1] == CTX_LEN == TOKEN_TILE
    assert seq % (2 * REV_BLOCK) == 0 and batch + 1 <= COND_ROWS

    cond = jnp.concatenate([c, c_ctx[None, :], jnp.zeros((COND_ROWS - batch - 1, D_MODEL), F32)], axis=0)
    mods = _adaln(cond, w_mod, b_mod)

    rope = _rope_tables(seq)
    ch, sh = (a.astype(BF16) for a in _dft_cos_sin(seq, seq // 2, seq // 2))
    cc, sc = _dft_cos_sin(CTX_LEN, CTX_LEN, CTX_LEN)
    wc = jnp.concatenate([cc, -sc], axis=1).astype(BF16)
    cg, sg = _dft_cos_sin(FOURIER_GROUP, FOURIER_GROUP, FOURIER_GROUP)
    csg = jnp.concatenate([cg, sg], axis=1).astype(BF16)
    rev = jnp.asarray(np.eye(REV_BLOCK)[::-1], F32).astype(BF16)

    w_in, w_pool, w_gate, w_br, w_out, w_ff1, w_ff2 = (
        w.astype(BF16) for w in (w_in, w_pool, w_gate, w_br, w_out, w_ff1, w_ff2))

    streams = (ctx, x)
    row = lambda a: a.reshape(1, -1)
    for l in range(depth):
        last = l == depth - 1
        lam_init = 0.8 - 0.6 * math.exp(-0.3 * l)
        m = mods[l]
        modsel = jnp.stack([jnp.broadcast_to(m[batch], (batch, N_MOD * D_MODEL)), m[:batch]],
                           axis=1)[:, :, None, :]
        lamvec = jnp.pad(jnp.stack([lam_q1[l], lam_k1[l], lam_q2[l], lam_k2[l]]),
                         ((0, SUBLANES - 4), (0, LANES - A_HEAD_DIM)))

        if len(streams) == 2:
            u, h = _inproj_split(*streams, modsel, row(norm1_g[l]), w_in, l, rope)
        else:
            u, h = _inproj_pair(*streams, modsel, row(norm1_g[l]), w_in, l, rope)
        ya = _attention(u, lamvec, row(jnp.tile(subln_g[l], A_HEADS)), lam_init, ctx_tile=not last)
        yf = _fourier(u, ch, sh, rev, wc, csg, ctx_tile=not last)
        xs = _merge(streams, h, u, ya, yf, modsel, conv_w[l], row(conv_b[l]), w_pool, row(pool_scale[l]),
                    w_gate, row(b_gate[l]), w_br, w_out, l, ctx_tile=not last)
        xs = _ffn(xs, modsel, row(norm2_g[l]), w_ff1, row(b_ff1[l]), w_ff2, row(b_ff2[l]), row(final_g),
                  l, latent_only=last, final=last)
        streams = (xs,)
    return xs
```

```python
import functools
import math

import jax
import jax.numpy as jnp
import numpy as np
from jax import lax
from jax.experimental import pallas as pl
from jax.experimental.pallas import tpu as pltpu

F32 = jnp.float32
BF16 = jnp.bfloat16

D_MODEL = 1024
CTX_LEN = 256
GRID_W = 64
N_BRANCH = 4
BRANCH_WIDTH = 512
A_HEADS = 4
A_HEAD_DIM = 64
HEAD_COLS = 2 * A_HEAD_DIM
AXIS_ROT = A_HEAD_DIM // 2
ROPE_BASE = 10000.0
ATTN_SCALE = A_HEAD_DIM ** -0.5
CONV_WIDTH = 3
FOURIER_GROUPS = 4
FOURIER_GROUP = BRANCH_WIDTH // FOURIER_GROUPS
POOL_WINDOWS = (2, 4, 8, 16)
POOL_GROUP = BRANCH_WIDTH // len(POOL_WINDOWS)
N_MOD = 6
NORM_EPS = 1e-6
IN_WIDTH = 8 * BRANCH_WIDTH
D_FF = 4 * D_MODEL

TOKEN_TILE = 256
PAIR = 2
HALO = 16
COND_ROWS = 16
LANES = 128
SUBLANES = 8
REV_BLOCK = 256
LOG2_E = math.log2(math.e)
VMEM_LIMIT = 52 * 1024 * 1024


def _dot(a, b):
    return jnp.dot(a, b, preferred_element_type=F32)


def _rms(x):
    return x * lax.rsqrt(jnp.mean(x * x, axis=-1, keepdims=True) + NORM_EPS)


def _params(*sem):
    return pltpu.CompilerParams(dimension_semantics=sem, vmem_limit_bytes=VMEM_LIMIT)


def _const_spec(shape):
    zeros = (0,) * len(shape)
    return pl.BlockSpec(shape, lambda *_: zeros)


def _layer_spec(shape, layer):
    zeros = (0,) * len(shape)
    return pl.BlockSpec((None,) + shape, lambda *_: (layer,) + zeros)


def _mod_spec(first):
    return pl.BlockSpec((None, None, 1, N_MOD * D_MODEL), lambda bb, i: (bb, jnp.minimum(i + first, 1), 0, 0))


def _stream_specs(split, width, first):
    if split:
        return [pl.BlockSpec((None, TOKEN_TILE, width), lambda bb, i: (bb, 0, 0)),
                pl.BlockSpec((None, TOKEN_TILE, width), lambda bb, i: (bb, jnp.maximum(i + first - 1, 0), 0))]
    return [pl.BlockSpec((None, TOKEN_TILE, width), lambda bb, i: (bb, i + first, 0))]


def _stream_tile(refs, i):
    if len(refs) == 2:
        return jnp.where(i == 0, refs[0][...], refs[1][...])
    return refs[0][...]


WIDE_STAGE = (2, 128, 4 * D_MODEL)
TALL_STAGE = (2, 512, D_MODEL)
_HBM = pl.BlockSpec(memory_space=pl.ANY)
_DMA_SEMS = pltpu.SemaphoreType.DMA((2,))


def _load_bf16(src_at, dst_at, n_chunks, stage, sem):
    copies = [pltpu.make_async_copy(src_at(c), stage.at[c % 2], sem.at[c % 2]) for c in range(n_chunks)]
    copies[0].start()
    for c in range(n_chunks):
        if c + 1 < n_chunks:
            copies[c + 1].start()
        copies[c].wait()
        dst_at(c)[...] = stage[c % 2].astype(BF16)


def _load_rows_bf16(w_hbm, layer, dst, stage, sem):
    rows = stage.shape[1]
    _load_bf16(lambda c: w_hbm.at[layer, c * rows:(c + 1) * rows, :],
               lambda c: dst.at[c * rows:(c + 1) * rows, :], dst.shape[0] // rows, stage, sem)


def _adaln_kernel(cond_ref, w_ref, b_ref, o_ref):
    c = cond_ref[...]
    s = c * jax.nn.sigmoid(c)
    o_ref[...] = _dot(s.astype(BF16), w_ref[...].astype(BF16)) + b_ref[...]


def _adaln(cond, w_mod, b_mod):
    depth = w_mod.shape[0]
    return pl.pallas_call(
        _adaln_kernel,
        grid=(depth, N_MOD),
        in_specs=[
            pl.BlockSpec((COND_ROWS, D_MODEL), lambda l, j: (0, 0)),
            pl.BlockSpec((None, D_MODEL, D_MODEL), lambda l, j: (l, 0, j)),
            pl.BlockSpec((None, 1, D_MODEL), lambda l, j: (l, 0, j)),
        ],
        out_specs=pl.BlockSpec((None, COND_ROWS, D_MODEL), lambda l, j: (l, 0, j)),
        out_shape=jax.ShapeDtypeStruct((depth, COND_ROWS, N_MOD * D_MODEL), F32),
        compiler_params=_params("arbitrary", "arbitrary"),
        name="adaln",
    )(cond, w_mod, b_mod.reshape(depth, 1, N_MOD * D_MODEL))


def _inproj_body(x_tiles, mod_refs, rope_refs, g_ref, w_ref, u_ref, h_ref):
    hs = []
    for x, mod_ref in zip(x_tiles, mod_refs):
        shift = mod_ref[:, 0:D_MODEL]
        scale = mod_ref[:, D_MODEL:2 * D_MODEL]
        hs.append((_rms(x) * g_ref[...] * (1.0 + scale) + shift).astype(BF16))
    hb = jnp.concatenate(hs, axis=0)
    h_ref[...] = hb
    for j in range(IN_WIDTH // BRANCH_WIDTH):
        cols = slice(j * BRANCH_WIDTH, (j + 1) * BRANCH_WIDTH)
        u = _dot(hb, w_ref[:, cols])
        if j < 2:
            tiles = []
            for k, (cos_ref, sup_ref, sdn_ref) in enumerate(rope_refs):
                parts = []
                for c in range(BRANCH_WIDTH // LANES):
                    uc = u[k * TOKEN_TILE:(k + 1) * TOKEN_TILE, c * LANES:(c + 1) * LANES]
                    up = pltpu.roll(uc, LANES - AXIS_ROT // 2, axis=1)
                    dn = pltpu.roll(uc, AXIS_ROT // 2, axis=1)
                    parts.append(uc * cos_ref[...] + up * sup_ref[...] + dn * sdn_ref[...])
                tiles.append(jnp.concatenate(parts, axis=1))
            u = jnp.concatenate(tiles, axis=0)
            if j == 0:
                u = u * (ATTN_SCALE * LOG2_E)
        u_ref[:, cols] = u.astype(BF16)


def _inproj_split_kernel(ctx_ref, x_ref, mod_ref, g_ref, w_hbm, cos_ref, sup_ref, sdn_ref, u_ref, h_ref,
                         w_ref, stage, sem, *, layer):
    @pl.when(jnp.logical_and(pl.program_id(0) == 0, pl.program_id(1) == 0))
    def _():
        _load_rows_bf16(w_hbm, layer, w_ref, stage, sem)

    x = _stream_tile((ctx_ref, x_ref), pl.program_id(1))
    _inproj_body([x], [mod_ref], [(cos_ref, sup_ref, sdn_ref)], g_ref, w_ref, u_ref, h_ref)


def _inproj_pair_kernel(*refs, layer):
    x_ref = refs[0]
    mod_refs = refs[1:1 + PAIR]
    g_ref, w_hbm = refs[1 + PAIR:3 + PAIR]
    rope_refs = [refs[3 + PAIR + 3 * k:6 + PAIR + 3 * k] for k in range(PAIR)]
    u_ref, h_ref, w_ref, stage, sem = refs[3 + 4 * PAIR:]

    @pl.when(pl.program_id(0) == 0)
    def _():
        _load_rows_bf16(w_hbm, layer, w_ref, stage, sem)

    x_tiles = [x_ref[k * TOKEN_TILE:(k + 1) * TOKEN_TILE, :] for k in range(PAIR)]
    _inproj_body(x_tiles, mod_refs, rope_refs, g_ref, w_ref, u_ref, h_ref)


_INPROJ_SCRATCH = [pltpu.VMEM((D_MODEL, IN_WIDTH), BF16), pltpu.VMEM(WIDE_STAGE, F32), _DMA_SEMS]


def _inproj_split(ctx, x, modsel, gain, w_in, layer, rope):
    b = x.shape[0]
    t = rope[0].shape[0]
    tile = lambda bb, i: (bb, i, 0)
    rope_spec = pl.BlockSpec((TOKEN_TILE, LANES), lambda bb, i: (i, 0))
    return pl.pallas_call(
        functools.partial(_inproj_split_kernel, layer=layer),
        grid=(b, t // TOKEN_TILE),
        in_specs=_stream_specs(True, D_MODEL, 0) + [
            _mod_spec(0),
            _const_spec((1, D_MODEL)),
            _HBM,
            rope_spec, rope_spec, rope_spec,
        ],
        out_specs=[
            pl.BlockSpec((None, TOKEN_TILE, IN_WIDTH), tile),
            pl.BlockSpec((None, TOKEN_TILE, D_MODEL), tile),
        ],
        out_shape=[
            jax.ShapeDtypeStruct((b, t, IN_WIDTH), BF16),
            jax.ShapeDtypeStruct((b, t, D_MODEL), BF16),
        ],
        scratch_shapes=_INPROJ_SCRATCH,
        compiler_params=_params("arbitrary", "arbitrary"),
        name="inproj",
    )(ctx, x, modsel, gain, w_in, *rope)


def _inproj_pair(xs, modsel, gain, w_in, layer, rope):
    b, t, _ = xs.shape
    nt = t // TOKEN_TILE
    assert (b * nt) % PAIR == 0

    def mod(k):
        def index(s):
            tile = s * PAIR + k
            return (tile // nt, jnp.minimum(tile % nt, 1), 0, 0)
        return pl.BlockSpec((None, None, 1, N_MOD * D_MODEL), index)

    def rope_specs(k):
        return [pl.BlockSpec((TOKEN_TILE, LANES), lambda s: ((s * PAIR + k) % nt, 0))] * 3

    rows = lambda width: pl.BlockSpec((PAIR * TOKEN_TILE, width), lambda s: (s, 0))
    u, h = pl.pallas_call(
        functools.partial(_inproj_pair_kernel, layer=layer),
        grid=(b * nt // PAIR,),
        in_specs=[rows(D_MODEL)] + [mod(k) for k in range(PAIR)] + [
            _const_spec((1, D_MODEL)),
            _HBM,
        ] + [spec for k in range(PAIR) for spec in rope_specs(k)],
        out_specs=[rows(IN_WIDTH), rows(D_MODEL)],
        out_shape=[
            jax.ShapeDtypeStruct((b * t, IN_WIDTH), BF16),
            jax.ShapeDtypeStruct((b * t, D_MODEL), BF16),
        ],
        scratch_shapes=_INPROJ_SCRATCH,
        compiler_params=_params("arbitrary"),
        name="inproj",
    )(xs.reshape(b * t, D_MODEL), *([modsel] * PAIR), gain, w_in, *(rope * PAIR))
    return u.reshape(b, t, IN_WIDTH), h.reshape(b, t, D_MODEL)


def _attn_kernel(q_ref, k_ref, v_ref, lam_ref, g_ref, o_ref, vx_ref, *, lam_init, ctx_tile):
    lv = lam_ref[...]
    lam = (jnp.exp(jnp.sum(lv[0:1] * lv[1:2], axis=1, keepdims=True))
           - jnp.exp(jnp.sum(lv[2:3] * lv[3:4], axis=1, keepdims=True)) + lam_init)
    t = k_ref.shape[0]

    @pl.when(pl.program_id(1) == 0)
    def _():
        for h in range(A_HEADS):
            vx_ref[:, 2 * h * HEAD_COLS:(2 * h + 1) * HEAD_COLS] = v_ref[:, h * HEAD_COLS:(h + 1) * HEAD_COLS]
            vx_ref[:, (2 * h + 1) * HEAD_COLS:(2 * h + 2) * HEAD_COLS] = jnp.ones((t, HEAD_COLS), BF16)

    def run(n_keys):
        lane = lax.broadcasted_iota(jnp.int32, (TOKEN_TILE, HEAD_COLS), 1)
        zero = jnp.zeros((TOKEN_TILE, HEAD_COLS), BF16)

        def scores(chain):
            h, m = divmod(chain, 2)
            cols = slice(h * HEAD_COLS, (h + 1) * HEAD_COLS)
            qm = jnp.where((lane >= A_HEAD_DIM) if m else (lane < A_HEAD_DIM), q_ref[:, cols], zero)
            return lax.dot_general(qm, k_ref[0:n_keys, cols], (((1,), (1,)), ((), ())),
                                   preferred_element_type=F32)

        n_chains = 2 * A_HEADS
        res = []
        s_next = scores(0)
        for chain in range(n_chains):
            s = s_next
            if chain + 1 < n_chains:
                s_next = scores(chain + 1)
            p = jnp.exp2(s - jnp.max(s, axis=-1, keepdims=True)).astype(BF16)
            h = chain // 2
            res.append(_dot(p, vx_ref[0:n_keys, 2 * h * HEAD_COLS:(2 * h + 2) * HEAD_COLS]))
        outs = []
        for h in range(A_HEADS):
            r1, r2 = res[2 * h], res[2 * h + 1]
            o = (r1[:, :HEAD_COLS] / r1[:, HEAD_COLS:] - lam * (r2[:, :HEAD_COLS] / r2[:, HEAD_COLS:]))
            outs.append(_rms(o))
        y = jnp.concatenate(outs, axis=1) * g_ref[...] * (1.0 - lam_init)
        o_ref[...] = y.astype(BF16)

    if ctx_tile:
        pl.when(pl.program_id(1) == 0)(lambda: run(CTX_LEN))
        pl.when(pl.program_id(1) > 0)(lambda: run(t))
    else:
        run(t)


def _attention(u, lamvec, subln_g, lam_init, ctx_tile):
    b, t, _ = u.shape
    first = 0 if ctx_tile else 1
    nq = t // TOKEN_TILE - first
    tile = pl.BlockSpec((None, TOKEN_TILE, BRANCH_WIDTH), lambda bb, i: (bb, i, 0))
    return pl.pallas_call(
        functools.partial(_attn_kernel, lam_init=lam_init, ctx_tile=ctx_tile),
        grid=(b, nq),
        in_specs=[
            pl.BlockSpec((None, TOKEN_TILE, BRANCH_WIDTH), lambda bb, i: (bb, i + first, 0)),
            pl.BlockSpec((None, t, BRANCH_WIDTH), lambda bb, i: (bb, 0, 1)),
            pl.BlockSpec((None, t, BRANCH_WIDTH), lambda bb, i: (bb, 0, 2)),
            _const_spec((SUBLANES, LANES)),
            _const_spec((1, BRANCH_WIDTH)),
        ],
        out_specs=tile,
        out_shape=jax.ShapeDtypeStruct((b, nq * TOKEN_TILE, BRANCH_WIDTH), BF16),
        scratch_shapes=[pltpu.VMEM((t, 2 * BRANCH_WIDTH), BF16)],
        compiler_params=_params("arbitrary", "arbitrary"),
        name="diff_attention",
    )(u, u, u, lamvec, subln_g)


def _fourier_kernel(f_ref, ch_ref, sh_ref, rev_ref, wc_ref, csg_ref, o_ref, buf, *, ctx_tile):
    t = f_ref.shape[0]
    seq = t - CTX_LEN
    m = seq // 2
    out0 = CTX_LEN if ctx_tile else 0

    def channel_dft(rows):
        zc, zs = [], []
        for g in range(FOURIER_GROUPS):
            zz = _dot(f_ref[rows, g * FOURIER_GROUP:(g + 1) * FOURIER_GROUP], csg_ref[...])
            zc.append(zz[:, :FOURIER_GROUP])
            zs.append(zz[:, FOURIER_GROUP:])
        return jnp.concatenate(zc, axis=1), jnp.concatenate(zs, axis=1)

    def reversed_shifted(a, row0):
        ab = a.astype(BF16)
        nb = m // REV_BLOCK
        for c in range(nb):
            buf[SUBLANES + c * REV_BLOCK:SUBLANES + (c + 1) * REV_BLOCK, :] = _dot(
                rev_ref[...], ab[(nb - 1 - c) * REV_BLOCK:(nb - c) * REV_BLOCK, :])
        buf[SUBLANES - 1:SUBLANES, :] = row0
        return buf[SUBLANES - 1:SUBLANES - 1 + m, :]

    if ctx_tile:
        zc, zs = channel_dft(slice(0, CTX_LEN))
        o_ref[0:CTX_LEN, :] = (_dot(wc_ref[:, :CTX_LEN], zc.astype(BF16))
                               + _dot(wc_ref[:, CTX_LEN:], zs.astype(BF16))).astype(BF16)

    zc, zs = channel_dft(slice(CTX_LEN, t))
    zero_row = jnp.zeros((1, BRANCH_WIDTH), F32)
    e = zc[:m] + reversed_shifted(zc[m:], zero_row)
    o = zs[:m] - reversed_shifted(zs[m:], zero_row)
    row = lax.broadcasted_iota(jnp.int32, (m, BRANCH_WIDTH), 0)
    sign = jnp.where((row & 1) == 0, 1.0, -1.0)
    nyq = zc[m:m + 1] * seq ** -0.5
    p = _dot(ch_ref[...], e.astype(BF16)) + sign * nyq
    q = _dot(sh_ref[...], o.astype(BF16))
    o_ref[out0:out0 + m, :] = (p - q).astype(BF16)
    y_mid = jnp.sum(sign * e, axis=0, keepdims=True) * seq ** -0.5 + nyq
    o_ref[out0 + m:out0 + seq, :] = reversed_shifted(p + q, y_mid).astype(BF16)


def _fourier(u, ch, sh, rev, wc, csg, ctx_tile):
    b, t, _ = u.shape
    seq = t - CTX_LEN
    out_t = t if ctx_tile else seq
    return pl.pallas_call(
        functools.partial(_fourier_kernel, ctx_tile=ctx_tile),
        grid=(b,),
        in_specs=[
            pl.BlockSpec((None, t, BRANCH_WIDTH), lambda bb: (bb, 0, 6)),
            _const_spec((seq // 2, seq // 2)),
            _const_spec((seq // 2, seq // 2)),
            _const_spec((REV_BLOCK, REV_BLOCK)),
            _const_spec((CTX_LEN, 2 * CTX_LEN)),
            _const_spec((FOURIER_GROUP, 2 * FOURIER_GROUP)),
        ],
        out_specs=pl.BlockSpec((None, out_t, BRANCH_WIDTH), lambda bb: (bb, 0, 0)),
        out_shape=jax.ShapeDtypeStruct((b, out_t, BRANCH_WIDTH), BF16),
        scratch_shapes=[pltpu.VMEM((seq // 2 + 2 * SUBLANES, BRANCH_WIDTH), F32)],
        compiler_params=_params("arbitrary"),
        name="fourier",
    )(u, ch, sh, rev, wc, csg)


def _merge_kernel(*refs, n_stream, first, n_tiles, layer):
    x_refs = refs[:n_stream]
    (h_ref, bg_ref, cg_ref, xs_ref, p_ref, cgp_ref, xsp_ref, pp_ref, cgn_ref, xsn_ref, pn_ref,
     ya_ref, yf_ref, mod_ref, convw_ref, convb_ref, wpool_ref, pscale_ref,
     wgate_hbm, bgate_ref, wbr_hbm, wout_hbm, o_ref,
     zbuf, pbuf, wgate_ref, wbr_ref, wout_ref, wide_stage, tall_stage, sem) = refs[n_stream:]

    @pl.when(jnp.logical_and(pl.program_id(0) == 0, pl.program_id(1) == 0))
    def _():
        _load_rows_bf16(wgate_hbm, layer, wgate_ref, wide_stage, sem)
        _load_bf16(lambda n: wbr_hbm.at[layer, n], lambda n: wbr_ref.at[n], N_BRANCH, tall_stage, sem)
        _load_rows_bf16(wout_hbm, layer, wout_ref, tall_stage, sem)

    i = pl.program_id(1) + first
    tm = TOKEN_TILE
    prev_ok = i >= 2
    next_ok = jnp.logical_and(i >= 1, i < n_tiles - 1)

    z = cg_ref[...].astype(F32) * xs_ref[...].astype(F32)
    zp = cgp_ref[...].astype(F32) * xsp_ref[...].astype(F32)
    zn = cgn_ref[...].astype(F32) * xsn_ref[...].astype(F32)
    zbuf[0:HALO, :] = jnp.where(prev_ok, zp, 0.0)
    zbuf[HALO:HALO + tm, :] = z
    zbuf[HALO + tm:, :] = jnp.where(next_ok, zn, 0.0)
    conv = (zbuf[HALO - 1:HALO - 1 + tm, :] * convw_ref[0:1, :] + z * convw_ref[1:2, :]
            + zbuf[HALO + 1:HALO + 1 + tm, :] * convw_ref[2:3, :] + convb_ref[...])
    y_conv = (bg_ref[...].astype(F32) * conv).astype(BF16)

    p = p_ref[...].astype(F32)
    pbuf[0:HALO, :] = jnp.where(prev_ok, pp_ref[...].astype(F32), 0.0)
    pbuf[HALO:HALO + tm, :] = p
    pbuf[HALO + tm:, :] = jnp.where(next_ok, pn_ref[...].astype(F32), 0.0)
    pos = (lax.broadcasted_iota(jnp.int32, (tm, POOL_GROUP), 0)
           + jnp.where(i == 0, 0, (i - 1) * tm))
    seq_len = jnp.where(i == 0, CTX_LEN, (n_tiles - 1) * tm)
    pool_parts = []
    for g, w in enumerate(POOL_WINDOWS):
        cols = slice(g * POOL_GROUP, (g + 1) * POOL_GROUP)
        win = pbuf[HALO - w // 2:HALO - w // 2 + tm, cols]
        for d in range(1, w):
            win = win + pbuf[HALO - w // 2 + d:HALO - w // 2 + d + tm, cols]
        lo = jnp.maximum(pos - w // 2, 0)
        hi = jnp.minimum(pos - w // 2 + w, seq_len)
        pooled = (win / (hi - lo).astype(F32) - p[:, cols]).astype(BF16)
        pool_parts.append(_dot(pooled, wpool_ref[g]) * pscale_ref[:, cols])
    y_pool = jnp.concatenate(pool_parts, axis=1).astype(BF16)

    ys = (ya_ref[...], y_conv, yf_ref[...], y_pool)
    hb = h_ref[...]
    acc = jnp.zeros((tm, D_MODEL), F32)
    for n in range(N_BRANCH):
        cols = slice(n * D_MODEL, (n + 1) * D_MODEL)
        gate = jax.nn.sigmoid(_dot(hb, wgate_ref[:, cols]) + bgate_ref[:, cols])
        acc = acc + gate * _dot(ys[n], wbr_ref[n])
    out = _dot(acc.astype(BF16), wout_ref[...])
    g1 = mod_ref[:, 2 * D_MODEL:3 * D_MODEL]
    o_ref[...] = _stream_tile(x_refs, i) + g1 * out


def _merge(streams, h, u, ya, yf, modsel, conv_w, conv_b, w_pool, pool_scale, w_gate, b_gate, w_br, w_out,
           layer, ctx_tile):
    b, t, _ = u.shape
    n_tiles = t // TOKEN_TILE
    first = 0 if ctx_tile else 1
    hb = TOKEN_TILE // HALO
    col = lambda j: pl.BlockSpec((None, TOKEN_TILE, BRANCH_WIDTH), lambda bb, i: (bb, i + first, j))
    prev = lambda j: pl.BlockSpec(
        (None, HALO, BRANCH_WIDTH), lambda bb, i: (bb, jnp.maximum((i + first) * hb - 1, 0), j))
    nxt = lambda j: pl.BlockSpec(
        (None, HALO, BRANCH_WIDTH), lambda bb, i: (bb, jnp.minimum((i + first + 1) * hb, t // HALO - 1), j))
    branch = pl.BlockSpec((None, TOKEN_TILE, BRANCH_WIDTH), lambda bb, i: (bb, i, 0))
    return pl.pallas_call(
        functools.partial(_merge_kernel, n_stream=len(streams), first=first, n_tiles=n_tiles, layer=layer),
        grid=(b, n_tiles - first),
        in_specs=_stream_specs(len(streams) == 2, D_MODEL, first) + [
            pl.BlockSpec((None, TOKEN_TILE, D_MODEL), lambda bb, i: (bb, i + first, 0)),
            col(3), col(4), col(5), col(7),
            prev(4), prev(5), prev(7), nxt(4), nxt(5), nxt(7),
            branch, branch,
            _mod_spec(first),
            _const_spec((CONV_WIDTH, BRANCH_WIDTH)),
            _const_spec((1, BRANCH_WIDTH)),
            _layer_spec((len(POOL_WINDOWS), POOL_GROUP, POOL_GROUP), layer),
            _const_spec((1, BRANCH_WIDTH)),
            _HBM,
            _const_spec((1, N_BRANCH * D_MODEL)),
            _HBM,
            _HBM,
        ],
        out_specs=pl.BlockSpec((None, TOKEN_TILE, D_MODEL), lambda bb, i: (bb, i, 0)),
        out_shape=jax.ShapeDtypeStruct((b, (n_tiles - first) * TOKEN_TILE, D_MODEL), F32),
        scratch_shapes=[
            pltpu.VMEM((TOKEN_TILE + 2 * HALO, BRANCH_WIDTH), F32),
            pltpu.VMEM((TOKEN_TILE + 2 * HALO, BRANCH_WIDTH), F32),
            pltpu.VMEM((D_MODEL, N_BRANCH * D_MODEL), BF16),
            pltpu.VMEM((N_BRANCH, BRANCH_WIDTH, D_MODEL), BF16),
            pltpu.VMEM((D_MODEL, D_MODEL), BF16),
            pltpu.VMEM(WIDE_STAGE, F32),
            pltpu.VMEM(TALL_STAGE, F32),
            _DMA_SEMS,
        ],
        compiler_params=_params("arbitrary", "arbitrary"),
        name="merge",
    )(*streams, h, u, u, u, u, u, u, u, u, u, u, ya, yf, modsel, conv_w, conv_b, w_pool,
      pool_scale, w_gate, b_gate, w_br, w_out)


def _ffn_kernel(*refs, final, layer):
    x_ref = refs[0]
    mod_refs = refs[1:1 + PAIR]
    (g_ref, w1_hbm, b1_ref, w2_hbm, b2_ref, fg_ref, o_ref,
     w1_ref, w2_ref, wide_stage, tall_stage, sem) = refs[1 + PAIR:]
    tiles = [(slice(k * TOKEN_TILE, (k + 1) * TOKEN_TILE), mod_refs[k]) for k in range(PAIR)]

    @pl.when(pl.program_id(0) == 0)
    def _():
        _load_rows_bf16(w1_hbm, layer, w1_ref, wide_stage, sem)
        _load_rows_bf16(w2_hbm, layer, w2_ref, tall_stage, sem)

    def normed(rows, mod_ref):
        shift = mod_ref[:, 3 * D_MODEL:4 * D_MODEL]
        scale = mod_ref[:, 4 * D_MODEL:5 * D_MODEL]
        return (_rms(x_ref[rows, :]) * g_ref[...] * (1.0 + scale) + shift).astype(BF16)

    hb = jnp.concatenate([normed(rows, mod_ref) for rows, mod_ref in tiles], axis=0)
    acc = jnp.zeros(x_ref.shape, F32)
    for c in range(D_FF // D_MODEL):
        cols = slice(c * D_MODEL, (c + 1) * D_MODEL)
        hid = jnp.square(jnp.maximum(_dot(hb, w1_ref[:, cols]) + b1_ref[:, cols], 0.0))
        acc = acc + _dot(hid.astype(BF16), w2_ref[cols, :])
    for rows, mod_ref in tiles:
        gate = mod_ref[:, 5 * D_MODEL:6 * D_MODEL]
        y = x_ref[rows, :] + gate * (acc[rows] + b2_ref[...])
        if final:
            y = _rms(y) * fg_ref[...]
        o_ref[rows, :] = y


def _ffn(xs, modsel, gain, w1, b1, w2, b2, final_g, layer, latent_only, final):
    b, rows, _ = xs.shape
    nt = rows // TOKEN_TILE
    first = 1 if latent_only else 0
    assert (b * nt) % PAIR == 0

    def mod(k):
        def index(s):
            tile = s * PAIR + k
            return (tile // nt, jnp.minimum(tile % nt + first, 1), 0, 0)
        return pl.BlockSpec((None, None, 1, N_MOD * D_MODEL), index)

    block = pl.BlockSpec((PAIR * TOKEN_TILE, D_MODEL), lambda s: (s, 0))
    out = pl.pallas_call(
        functools.partial(_ffn_kernel, final=final, layer=layer),
        grid=(b * nt // PAIR,),
        in_specs=[block] + [mod(k) for k in range(PAIR)] + [
            _const_spec((1, D_MODEL)),
            _HBM,
            _const_spec((1, D_FF)),
            _HBM,
            _const_spec((1, D_MODEL)),
            _const_spec((1, D_MODEL)),
        ],
        out_specs=block,
        out_shape=jax.ShapeDtypeStruct((b * rows, D_MODEL), F32),
        scratch_shapes=[
            pltpu.VMEM((D_MODEL, D_FF), BF16),
            pltpu.VMEM((D_FF, D_MODEL), BF16),
            pltpu.VMEM(WIDE_STAGE, F32),
            pltpu.VMEM(TALL_STAGE, F32),
            _DMA_SEMS,
        ],
        compiler_params=_params("arbitrary"),
        name="ffn",
    )(xs.reshape(b * rows, D_MODEL), *([modsel] * PAIR), gain, w1, b1, w2, b2, final_g)
    return out.reshape(b, rows, D_MODEL)


def _rope_tables(seq):
    lane = np.arange(LANES)
    d = lane % A_HEAD_DIM
    axis = d // AXIS_ROT
    upper = (d % AXIS_ROT) // (AXIS_ROT // 2)
    inv = ROPE_BASE ** (-(d % (AXIS_ROT // 2)) * 2.0 / AXIS_ROT)
    tok = np.arange(seq)
    pos = np.where(axis[None, :] == 0, (tok // GRID_W)[:, None], (tok % GRID_W)[:, None])
    ang = pos * inv[None, :]
    cos, sin = np.cos(ang), np.sin(ang)
    s_up = np.where(upper[None, :] == 0, -sin, 0.0)
    s_dn = np.where(upper[None, :] == 1, sin, 0.0)
    pad = lambda a, v: jnp.asarray(np.concatenate([np.full((CTX_LEN, LANES), v), a], axis=0), F32)
    return pad(cos, 1.0), pad(s_up, 0.0), pad(s_dn, 0.0)


def _dft_cos_sin(n, rows, cols):
    ang = (np.arange(rows)[:, None] * np.arange(cols)[None, :] % n) * (2.0 * math.pi / n)
    return jnp.asarray(np.cos(ang) * n ** -0.5, F32), jnp.asarray(np.sin(ang) * n ** -0.5, F32)


def kernel(x, c, ctx, c_ctx, w_mod, b_mod, norm1_g, w_in, lam_q1, lam_k1, lam_q2, lam_k2, subln_g, conv_w,
           conv_b, w_pool, pool_scale, w_gate, b_gate, w_br, w_out, norm2_g, w_ff1, b_ff1, w_ff2, b_ff2,
           final_g):
    batch, seq, d_model = x.shape
    depth = w_mod.shape[0]
    assert d_model == D_MODEL and ctx.shape[1] == CTX_LEN == TOKEN_TILE
    assert seq % (2 * REV_BLOCK) == 0 and batch + 1 <= COND_ROWS

    cond = jnp.concatenate([c, c_ctx[None, :], jnp.zeros((COND_ROWS - batch - 1, D_MODEL), F32)], axis=0)
    mods = _adaln(cond, w_mod, b_mod)

    rope = _rope_tables(seq)
    ch, sh = (a.astype(BF16) for a in _dft_cos_sin(seq, seq // 2, seq // 2))
    cc, sc = _dft_cos_sin(CTX_LEN, CTX_LEN, CTX_LEN)
    wc = jnp.concatenate([cc, -sc], axis=1).astype(BF16)
    cg, sg = _dft_cos_sin(FOURIER_GROUP, FOURIER_GROUP, FOURIER_GROUP)
    csg = jnp.concatenate([cg, sg], axis=1).astype(BF16)
    rev = jnp.asarray(np.eye(REV_BLOCK)[::-1], F32).astype(BF16)

    w_pool = w_pool.astype(BF16)

    streams = (ctx, x)
    row = lambda a: a.reshape(1, -1)
    for l in range(depth):
        last = l == depth - 1
        lam_init = 0.8 - 0.6 * math.exp(-0.3 * l)
        m = mods[l]
        modsel = jnp.stack([jnp.broadcast_to(m[batch], (batch, N_MOD * D_MODEL)), m[:batch]],
                           axis=1)[:, :, None, :]
        lamvec = jnp.pad(jnp.stack([lam_q1[l], lam_k1[l], lam_q2[l], lam_k2[l]]),
                         ((0, SUBLANES - 4), (0, LANES - A_HEAD_DIM)))

        if len(streams) == 2:
            u, h = _inproj_split(*streams, modsel, row(norm1_g[l]), w_in, l, rope)
        else:
            u, h = _inproj_pair(*streams, modsel, row(norm1_g[l]), w_in, l, rope)
        ya = _attention(u, lamvec, row(jnp.tile(subln_g[l], A_HEADS)), lam_init, ctx_tile=not last)
        yf = _fourier(u, ch, sh, rev, wc, csg, ctx_tile=not last)
        xs = _merge(streams, h, u, ya, yf, modsel, conv_w[l], row(conv_b[l]), w_pool, row(pool_scale[l]),
                    w_gate, row(b_gate[l]), w_br, w_out, l, ctx_tile=not last)
        xs = _ffn(xs, modsel, row(norm2_g[l]), w_ff1, row(b_ff1[l]), w_ff2, row(b_ff2[l]), row(final_g),
                  l, latent_only=last, final=last)
        streams = (xs,)
    return xs
```

```python
import functools
import math

import jax
import jax.numpy as jnp
import numpy as np
from jax import lax
from jax.experimental import pallas as pl
from jax.experimental.pallas import tpu as pltpu

F32 = jnp.float32
BF16 = jnp.bfloat16

D_MODEL = 1024
CTX_LEN = 256
GRID_W = 64
N_BRANCH = 4
BRANCH_WIDTH = 512
A_HEADS = 4
A_HEAD_DIM = 64
HEAD_COLS = 2 * A_HEAD_DIM
AXIS_ROT = A_HEAD_DIM // 2
ROPE_BASE = 10000.0
ATTN_SCALE = A_HEAD_DIM ** -0.5
CONV_WIDTH = 3
FOURIER_GROUPS = 4
FOURIER_GROUP = BRANCH_WIDTH // FOURIER_GROUPS
POOL_WINDOWS = (2, 4, 8, 16)
POOL_GROUP = BRANCH_WIDTH // len(POOL_WINDOWS)
N_MOD = 6
NORM_EPS = 1e-6
IN_WIDTH = 8 * BRANCH_WIDTH
D_FF = 4 * D_MODEL

TOKEN_TILE = 256
PAIR = 2
HALO = 16
COND_ROWS = 16
LANES = 128
SUBLANES = 8
REV_BLOCK = 256
LOG2_E = math.log2(math.e)
VMEM_LIMIT = 52 * 1024 * 1024


def _dot(a, b):
    return jnp.dot(a, b, preferred_element_type=F32)


def _rms(x):
    return x * lax.rsqrt(jnp.mean(x * x, axis=-1, keepdims=True) + NORM_EPS)


def _params(*sem):
    return pltpu.CompilerParams(dimension_semantics=sem, vmem_limit_bytes=VMEM_LIMIT)


def _const_spec(shape):
    zeros = (0,) * len(shape)
    return pl.BlockSpec(shape, lambda *_: zeros)


def _layer_spec(shape, layer):
    zeros = (0,) * len(shape)
    return pl.BlockSpec((None,) + shape, lambda *_: (layer,) + zeros)


def _mod_spec(first):
    return pl.BlockSpec((None, None, 1, N_MOD * D_MODEL), lambda bb, i: (bb, jnp.minimum(i + first, 1), 0, 0))


def _stream_specs(split, width, first):
    if split:
        return [pl.BlockSpec((None, TOKEN_TILE, width), lambda bb, i: (bb, 0, 0)),
                pl.BlockSpec((None, TOKEN_TILE, width), lambda bb, i: (bb, jnp.maximum(i + first - 1, 0), 0))]
    return [pl.BlockSpec((None, TOKEN_TILE, width), lambda bb, i: (bb, i + first, 0))]


def _stream_tile(refs, i):
    if len(refs) == 2:
        return jnp.where(i == 0, refs[0][...], refs[1][...])
    return refs[0][...]


WIDE_STAGE = (2, 128, 4 * D_MODEL)
TALL_STAGE = (2, 512, D_MODEL)
_HBM = pl.BlockSpec(memory_space=pl.ANY)
_DMA_SEMS = pltpu.SemaphoreType.DMA((2,))


def _load_bf16(src_at, dst_at, n_chunks, stage, sem):
    copies = [pltpu.make_async_copy(src_at(c), stage.at[c % 2], sem.at[c % 2]) for c in range(n_chunks)]
    copies[0].start()
    for c in range(n_chunks):
        if c + 1 < n_chunks:
            copies[c + 1].start()
        copies[c].wait()
        dst_at(c)[...] = stage[c % 2].astype(BF16)


def _load_rows_bf16(w_hbm, layer, dst, stage, sem):
    rows = stage.shape[1]
    _load_bf16(lambda c: w_hbm.at[layer, c * rows:(c + 1) * rows, :],
               lambda c: dst.at[c * rows:(c + 1) * rows, :], dst.shape[0] // rows, stage, sem)


def _adaln_kernel(cond_ref, w_ref, b_ref, o_ref):
    c = cond_ref[...]
    s = c * jax.nn.sigmoid(c)
    o_ref[...] = _dot(s.astype(BF16), w_ref[...].astype(BF16)) + b_ref[...]


def _adaln(cond, w_mod, b_mod):
    depth = w_mod.shape[0]
    return pl.pallas_call(
        _adaln_kernel,
        grid=(depth, N_MOD),
        in_specs=[
            pl.BlockSpec((COND_ROWS, D_MODEL), lambda l, j: (0, 0)),
            pl.BlockSpec((None, D_MODEL, D_MODEL), lambda l, j: (l, 0, j)),
            pl.BlockSpec((None, 1, D_MODEL), lambda l, j: (l, 0, j)),
        ],
        out_specs=pl.BlockSpec((None, COND_ROWS, D_MODEL), lambda l, j: (l, 0, j)),
        out_shape=jax.ShapeDtypeStruct((depth, COND_ROWS, N_MOD * D_MODEL), F32),
        compiler_params=_params("arbitrary", "arbitrary"),
        name="adaln",
    )(cond, w_mod, b_mod.reshape(depth, 1, N_MOD * D_MODEL))


def _inproj_body(x_tiles, mod_refs, rope_refs, g_ref, w_ref, u_ref, h_ref):
    hs = []
    for x, mod_ref in zip(x_tiles, mod_refs):
        shift = mod_ref[:, 0:D_MODEL]
        scale = mod_ref[:, D_MODEL:2 * D_MODEL]
        hs.append((_rms(x) * g_ref[...] * (1.0 + scale) + shift).astype(BF16))
    hb = jnp.concatenate(hs, axis=0)
    h_ref[...] = hb
    for j in range(IN_WIDTH // BRANCH_WIDTH):
        cols = slice(j * BRANCH_WIDTH, (j + 1) * BRANCH_WIDTH)
        u = _dot(hb, w_ref[:, cols])
        if j < 2:
            tiles = []
            for k, (cos_ref, sup_ref, sdn_ref) in enumerate(rope_refs):
                parts = []
                for c in range(BRANCH_WIDTH // LANES):
                    uc = u[k * TOKEN_TILE:(k + 1) * TOKEN_TILE, c * LANES:(c + 1) * LANES]
                    up = pltpu.roll(uc, LANES - AXIS_ROT // 2, axis=1)
                    dn = pltpu.roll(uc, AXIS_ROT // 2, axis=1)
                    parts.append(uc * cos_ref[...] + up * sup_ref[...] + dn * sdn_ref[...])
                tiles.append(jnp.concatenate(parts, axis=1))
            u = jnp.concatenate(tiles, axis=0)
            if j == 0:
                u = u * (ATTN_SCALE * LOG2_E)
        u_ref[:, cols] = u.astype(BF16)


def _inproj_split_kernel(ctx_ref, x_ref, mod_ref, g_ref, w_hbm, cos_ref, sup_ref, sdn_ref, u_ref, h_ref,
                         w_ref, stage, sem, *, layer):
    @pl.when(jnp.logical_and(pl.program_id(0) == 0, pl.program_id(1) == 0))
    def _():
        _load_rows_bf16(w_hbm, layer, w_ref, stage, sem)

    x = _stream_tile((ctx_ref, x_ref), pl.program_id(1))
    _inproj_body([x], [mod_ref], [(cos_ref, sup_ref, sdn_ref)], g_ref, w_ref, u_ref, h_ref)


def _inproj_pair_kernel(*refs, layer):
    x_ref = refs[0]
    mod_refs = refs[1:1 + PAIR]
    g_ref, w_hbm = refs[1 + PAIR:3 + PAIR]
    rope_refs = [refs[3 + PAIR + 3 * k:6 + PAIR + 3 * k] for k in range(PAIR)]
    u_ref, h_ref, w_ref, stage, sem = refs[3 + 4 * PAIR:]

    @pl.when(pl.program_id(0) == 0)
    def _():
        _load_rows_bf16(w_hbm, layer, w_ref, stage, sem)

    x_tiles = [x_ref[k * TOKEN_TILE:(k + 1) * TOKEN_TILE, :] for k in range(PAIR)]
    _inproj_body(x_tiles, mod_refs, rope_refs, g_ref, w_ref, u_ref, h_ref)


_INPROJ_SCRATCH = [pltpu.VMEM((D_MODEL, IN_WIDTH), BF16), pltpu.VMEM(WIDE_STAGE, F32), _DMA_SEMS]


def _inproj_split(ctx, x, modsel, gain, w_in, layer, rope):
    b = x.shape[0]
    t = rope[0].shape[0]
    tile = lambda bb, i: (bb, i, 0)
    rope_spec = pl.BlockSpec((TOKEN_TILE, LANES), lambda bb, i: (i, 0))
    return pl.pallas_call(
        functools.partial(_inproj_split_kernel, layer=layer),
        grid=(b, t // TOKEN_TILE),
        in_specs=_stream_specs(True, D_MODEL, 0) + [
            _mod_spec(0),
            _const_spec((1, D_MODEL)),
            _HBM,
            rope_spec, rope_spec, rope_spec,
        ],
        out_specs=[
            pl.BlockSpec((None, TOKEN_TILE, IN_WIDTH), tile),
            pl.BlockSpec((None, TOKEN_TILE, D_MODEL), tile),
        ],
        out_shape=[
            jax.ShapeDtypeStruct((b, t, IN_WIDTH), BF16),
            jax.ShapeDtypeStruct((b, t, D_MODEL), BF16),
        ],
        scratch_shapes=_INPROJ_SCRATCH,
        compiler_params=_params("arbitrary", "arbitrary"),
        name="inproj",
    )(ctx, x, modsel, gain, w_in, *rope)


def _inproj_pair(xs, modsel, gain, w_in, layer, rope):
    b, t, _ = xs.shape
    nt = t // TOKEN_TILE
    assert (b * nt) % PAIR == 0

    def mod(k):
        def index(s):
            tile = s * PAIR + k
            return (tile // nt, jnp.minimum(tile % nt, 1), 0, 0)
        return pl.BlockSpec((None, None, 1, N_MOD * D_MODEL), index)

    def rope_specs(k):
        return [pl.BlockSpec((TOKEN_TILE, LANES), lambda s: ((s * PAIR + k) % nt, 0))] * 3

    rows = lambda width: pl.BlockSpec((PAIR * TOKEN_TILE, width), lambda s: (s, 0))
    u, h = pl.pallas_call(
        functools.partial(_inproj_pair_kernel, layer=layer),
        grid=(b * nt // PAIR,),
        in_specs=[rows(D_MODEL)] + [mod(k) for k in range(PAIR)] + [
            _const_spec((1, D_MODEL)),
            _HBM,
        ] + [spec for k in range(PAIR) for spec in rope_specs(k)],
        out_specs=[rows(IN_WIDTH), rows(D_MODEL)],
        out_shape=[
            jax.ShapeDtypeStruct((b * t, IN_WIDTH), BF16),
            jax.ShapeDtypeStruct((b * t, D_MODEL), BF16),
        ],
        scratch_shapes=_INPROJ_SCRATCH,
        compiler_params=_params("arbitrary"),
        name="inproj",
    )(xs.reshape(b * t, D_MODEL), *([modsel] * PAIR), gain, w_in, *(rope * PAIR))
    return u.reshape(b, t, IN_WIDTH), h.reshape(b, t, D_MODEL)


def _attn_kernel(*refs, lam_init, ctx_tile):
    q_refs = refs[:PAIR]
    k_ref, v_ref, lam_ref, g_ref = refs[PAIR:PAIR + 4]
    out_refs, vx_ref = refs[PAIR + 4:-1], refs[-1]
    lv = lam_ref[...]
    lam = (jnp.exp(jnp.sum(lv[0:1] * lv[1:2], axis=1, keepdims=True))
           - jnp.exp(jnp.sum(lv[2:3] * lv[3:4], axis=1, keepdims=True)) + lam_init)
    t = k_ref.shape[0]

    @pl.when(pl.program_id(1) == 0)
    def _():
        for h in range(A_HEADS):
            vx_ref[:, 2 * h * HEAD_COLS:(2 * h + 1) * HEAD_COLS] = v_ref[:, h * HEAD_COLS:(h + 1) * HEAD_COLS]
            vx_ref[:, (2 * h + 1) * HEAD_COLS:(2 * h + 2) * HEAD_COLS] = jnp.ones((t, HEAD_COLS), BF16)

    def run(q_refs, n_keys, o_ref):
        rows = len(q_refs) * TOKEN_TILE
        lane = lax.broadcasted_iota(jnp.int32, (rows, HEAD_COLS), 1)
        zero = jnp.zeros((rows, HEAD_COLS), BF16)

        def scores(chain):
            h, m = divmod(chain, 2)
            cols = slice(h * HEAD_COLS, (h + 1) * HEAD_COLS)
            q = jnp.concatenate([q_ref[:, cols] for q_ref in q_refs], axis=0)
            qm = jnp.where((lane >= A_HEAD_DIM) if m else (lane < A_HEAD_DIM), q, zero)
            return lax.dot_general(qm, k_ref[0:n_keys, cols], (((1,), (1,)), ((), ())),
                                   preferred_element_type=F32)

        n_chains = 2 * A_HEADS
        res = []
        s_next = scores(0)
        for chain in range(n_chains):
            s = s_next
            if chain + 1 < n_chains:
                s_next = scores(chain + 1)
            p = jnp.exp2(s - jnp.max(s, axis=-1, keepdims=True)).astype(BF16)
            h = chain // 2
            res.append(_dot(p, vx_ref[0:n_keys, 2 * h * HEAD_COLS:(2 * h + 2) * HEAD_COLS]))
        outs = []
        for h in range(A_HEADS):
            r1, r2 = res[2 * h], res[2 * h + 1]
            o = (r1[:, :HEAD_COLS] / r1[:, HEAD_COLS:] - lam * (r2[:, :HEAD_COLS] / r2[:, HEAD_COLS:]))
            outs.append(_rms(o))
        y = jnp.concatenate(outs, axis=1) * g_ref[...] * (1.0 - lam_init)
        o_ref[...] = y.astype(BF16)

    if ctx_tile:
        pl.when(pl.program_id(1) == 0)(lambda: run(q_refs[:1], CTX_LEN, out_refs[0]))
        pl.when(pl.program_id(1) > 0)(lambda: run(q_refs, t, out_refs[1]))
    else:
        run(q_refs, t, out_refs[0])


def _attention(u, lamvec, subln_g, lam_init, ctx_tile):
    b, t, _ = u.shape
    n_pairs = (t - CTX_LEN) // (PAIR * TOKEN_TILE)
    lead = 1 if ctx_tile else 0

    def q_spec(k):
        def index(bb, i):
            return (bb, jnp.maximum((i - lead) * PAIR + 1 + k, k), 0)
        return pl.BlockSpec((None, TOKEN_TILE, BRANCH_WIDTH), index)

    latent = pl.BlockSpec((None, PAIR * TOKEN_TILE, BRANCH_WIDTH), lambda bb, i: (bb, jnp.maximum(i - lead, 0), 0))
    latent_shape = jax.ShapeDtypeStruct((b, t - CTX_LEN, BRANCH_WIDTH), BF16)
    context = pl.BlockSpec((None, CTX_LEN, BRANCH_WIDTH), lambda bb, i: (bb, 0, 0))
    context_shape = jax.ShapeDtypeStruct((b, CTX_LEN, BRANCH_WIDTH), BF16)
    return pl.pallas_call(
        functools.partial(_attn_kernel, lam_init=lam_init, ctx_tile=ctx_tile),
        grid=(b, n_pairs + lead),
        in_specs=[q_spec(k) for k in range(PAIR)] + [
            pl.BlockSpec((None, t, BRANCH_WIDTH), lambda bb, i: (bb, 0, 1)),
            pl.BlockSpec((None, t, BRANCH_WIDTH), lambda bb, i: (bb, 0, 2)),
            _const_spec((SUBLANES, LANES)),
            _const_spec((1, BRANCH_WIDTH)),
        ],
        out_specs=[context, latent] if ctx_tile else [latent],
        out_shape=[context_shape, latent_shape] if ctx_tile else [latent_shape],
        scratch_shapes=[pltpu.VMEM((t, 2 * BRANCH_WIDTH), BF16)],
        compiler_params=_params("arbitrary", "arbitrary"),
        name="diff_attention",
    )(*([u] * PAIR), u, u, lamvec, subln_g)


def _fourier_kernel(f_ref, ch_ref, sh_ref, rev_ref, wc_ref, csg_ref, o_ref, buf, *, ctx_tile):
    t = f_ref.shape[0]
    seq = t - CTX_LEN
    m = seq // 2
    out0 = CTX_LEN if ctx_tile else 0

    def channel_dft(rows):
        zc, zs = [], []
        for g in range(FOURIER_GROUPS):
            zz = _dot(f_ref[rows, g * FOURIER_GROUP:(g + 1) * FOURIER_GROUP], csg_ref[...])
            zc.append(zz[:, :FOURIER_GROUP])
            zs.append(zz[:, FOURIER_GROUP:])
        return jnp.concatenate(zc, axis=1), jnp.concatenate(zs, axis=1)

    def reversed_shifted(a, row0):
        ab = a.astype(BF16)
        nb = m // REV_BLOCK
        for c in range(nb):
            buf[SUBLANES + c * REV_BLOCK:SUBLANES + (c + 1) * REV_BLOCK, :] = _dot(
                rev_ref[...], ab[(nb - 1 - c) * REV_BLOCK:(nb - c) * REV_BLOCK, :])
        buf[SUBLANES - 1:SUBLANES, :] = row0
        return buf[SUBLANES - 1:SUBLANES - 1 + m, :]

    if ctx_tile:
        zc, zs = channel_dft(slice(0, CTX_LEN))
        o_ref[0:CTX_LEN, :] = (_dot(wc_ref[:, :CTX_LEN], zc.astype(BF16))
                               + _dot(wc_ref[:, CTX_LEN:], zs.astype(BF16))).astype(BF16)

    zc, zs = channel_dft(slice(CTX_LEN, t))
    zero_row = jnp.zeros((1, BRANCH_WIDTH), F32)
    e = zc[:m] + reversed_shifted(zc[m:], zero_row)
    o = zs[:m] - reversed_shifted(zs[m:], zero_row)
    row = lax.broadcasted_iota(jnp.int32, (m, BRANCH_WIDTH), 0)
    sign = jnp.where((row & 1) == 0, 1.0, -1.0)
    nyq = zc[m:m + 1] * seq ** -0.5
    p = _dot(ch_ref[...], e.astype(BF16)) + sign * nyq
    q = _dot(sh_ref[...], o.astype(BF16))
    o_ref[out0:out0 + m, :] = (p - q).astype(BF16)
    y_mid = jnp.sum(sign * e, axis=0, keepdims=True) * seq ** -0.5 + nyq
    o_ref[out0 + m:out0 + seq, :] = reversed_shifted(p + q, y_mid).astype(BF16)


def _fourier(u, ch, sh, rev, wc, csg, ctx_tile):
    b, t, _ = u.shape
    seq = t - CTX_LEN
    out_t = t if ctx_tile else seq
    return pl.pallas_call(
        functools.partial(_fourier_kernel, ctx_tile=ctx_tile),
        grid=(b,),
        in_specs=[
            pl.BlockSpec((None, t, BRANCH_WIDTH), lambda bb: (bb, 0, 6)),
            _const_spec((seq // 2, seq // 2)),
            _const_spec((seq // 2, seq // 2)),
            _const_spec((REV_BLOCK, REV_BLOCK)),
            _const_spec((CTX_LEN, 2 * CTX_LEN)),
            _const_spec((FOURIER_GROUP, 2 * FOURIER_GROUP)),
        ],
        out_specs=pl.BlockSpec((None, out_t, BRANCH_WIDTH), lambda bb: (bb, 0, 0)),
        out_shape=jax.ShapeDtypeStruct((b, out_t, BRANCH_WIDTH), BF16),
        scratch_shapes=[pltpu.VMEM((seq // 2 + 2 * SUBLANES, BRANCH_WIDTH), F32)],
        compiler_params=_params("arbitrary"),
        name="fourier",
    )(u, ch, sh, rev, wc, csg)


def _merge_kernel(*refs, n_stream, first, n_tiles, layer):
    x_refs, ya_refs = refs[:n_stream], refs[n_stream:2 * n_stream]
    (h_ref, bg_ref, cg_ref, xs_ref, p_ref, cgp_ref, xsp_ref, pp_ref, cgn_ref, xsn_ref, pn_ref,
     yf_ref, mod_ref, convw_ref, convb_ref, wpool_ref, pscale_ref,
     wgate_hbm, bgate_ref, wbr_hbm, wout_hbm, o_ref,
     zbuf, pbuf, wgate_ref, wbr_ref, wout_ref, wide_stage, tall_stage, sem) = refs[2 * n_stream:]

    @pl.when(jnp.logical_and(pl.program_id(0) == 0, pl.program_id(1) == 0))
    def _():
        _load_rows_bf16(wgate_hbm, layer, wgate_ref, wide_stage, sem)
        _load_bf16(lambda n: wbr_hbm.at[layer, n], lambda n: wbr_ref.at[n], N_BRANCH, tall_stage, sem)
        _load_rows_bf16(wout_hbm, layer, wout_ref, tall_stage, sem)

    i = pl.program_id(1) + first
    tm = TOKEN_TILE
    prev_ok = i >= 2
    next_ok = jnp.logical_and(i >= 1, i < n_tiles - 1)

    z = cg_ref[...].astype(F32) * xs_ref[...].astype(F32)
    zp = cgp_ref[...].astype(F32) * xsp_ref[...].astype(F32)
    zn = cgn_ref[...].astype(F32) * xsn_ref[...].astype(F32)
    zbuf[0:HALO, :] = jnp.where(prev_ok, zp, 0.0)
    zbuf[HALO:HALO + tm, :] = z
    zbuf[HALO + tm:, :] = jnp.where(next_ok, zn, 0.0)
    conv = (zbuf[HALO - 1:HALO - 1 + tm, :] * convw_ref[0:1, :] + z * convw_ref[1:2, :]
            + zbuf[HALO + 1:HALO + 1 + tm, :] * convw_ref[2:3, :] + convb_ref[...])
    y_conv = (bg_ref[...].astype(F32) * conv).astype(BF16)

    p = p_ref[...].astype(F32)
    pbuf[0:HALO, :] = jnp.where(prev_ok, pp_ref[...].astype(F32), 0.0)
    pbuf[HALO:HALO + tm, :] = p
    pbuf[HALO + tm:, :] = jnp.where(next_ok, pn_ref[...].astype(F32), 0.0)
    pos = (lax.broadcasted_iota(jnp.int32, (tm, POOL_GROUP), 0)
           + jnp.where(i == 0, 0, (i - 1) * tm))
    seq_len = jnp.where(i == 0, CTX_LEN, (n_tiles - 1) * tm)
    pool_parts = []
    for g, w in enumerate(POOL_WINDOWS):
        cols = slice(g * POOL_GROUP, (g + 1) * POOL_GROUP)
        win = pbuf[HALO - w // 2:HALO - w // 2 + tm, cols]
        for d in range(1, w):
            win = win + pbuf[HALO - w // 2 + d:HALO - w // 2 + d + tm, cols]
        lo = jnp.maximum(pos - w // 2, 0)
        hi = jnp.minimum(pos - w // 2 + w, seq_len)
        pooled = (win / (hi - lo).astype(F32) - p[:, cols]).astype(BF16)
        pool_parts.append(_dot(pooled, wpool_ref[g]) * pscale_ref[:, cols])
    y_pool = jnp.concatenate(pool_parts, axis=1).astype(BF16)

    ys = (_stream_tile(ya_refs, i), y_conv, yf_ref[...], y_pool)
    hb = h_ref[...]
    acc = jnp.zeros((tm, D_MODEL), F32)
    for n in range(N_BRANCH):
        cols = slice(n * D_MODEL, (n + 1) * D_MODEL)
        gate = jax.nn.sigmoid(_dot(hb, wgate_ref[:, cols]) + bgate_ref[:, cols])
        acc = acc + gate * _dot(ys[n], wbr_ref[n])
    out = _dot(acc.astype(BF16), wout_ref[...])
    g1 = mod_ref[:, 2 * D_MODEL:3 * D_MODEL]
    o_ref[...] = _stream_tile(x_refs, i) + g1 * out


def _merge(streams, h, u, ya, yf, modsel, conv_w, conv_b, w_pool, pool_scale, w_gate, b_gate, w_br, w_out,
           layer, ctx_tile):
    assert len(streams) == len(ya)
    b, t, _ = u.shape
    n_tiles = t // TOKEN_TILE
    first = 0 if ctx_tile else 1
    hb = TOKEN_TILE // HALO
    col = lambda j: pl.BlockSpec((None, TOKEN_TILE, BRANCH_WIDTH), lambda bb, i: (bb, i + first, j))
    prev = lambda j: pl.BlockSpec(
        (None, HALO, BRANCH_WIDTH), lambda bb, i: (bb, jnp.maximum((i + first) * hb - 1, 0), j))
    nxt = lambda j: pl.BlockSpec(
        (None, HALO, BRANCH_WIDTH), lambda bb, i: (bb, jnp.minimum((i + first + 1) * hb, t // HALO - 1), j))
    branch = pl.BlockSpec((None, TOKEN_TILE, BRANCH_WIDTH), lambda bb, i: (bb, i, 0))
    return pl.pallas_call(
        functools.partial(_merge_kernel, n_stream=len(streams), first=first, n_tiles=n_tiles, layer=layer),
        grid=(b, n_tiles - first),
        in_specs=_stream_specs(len(streams) == 2, D_MODEL, first)
        + _stream_specs(len(ya) == 2, BRANCH_WIDTH, 0) + [
            pl.BlockSpec((None, TOKEN_TILE, D_MODEL), lambda bb, i: (bb, i + first, 0)),
            col(3), col(4), col(5), col(7),
            prev(4), prev(5), prev(7), nxt(4), nxt(5), nxt(7),
            branch,
            _mod_spec(first),
            _const_spec((CONV_WIDTH, BRANCH_WIDTH)),
            _const_spec((1, BRANCH_WIDTH)),
            _layer_spec((len(POOL_WINDOWS), POOL_GROUP, POOL_GROUP), layer),
            _const_spec((1, BRANCH_WIDTH)),
            _HBM,
            _const_spec((1, N_BRANCH * D_MODEL)),
            _HBM,
            _HBM,
        ],
        out_specs=pl.BlockSpec((None, TOKEN_TILE, D_MODEL), lambda bb, i: (bb, i, 0)),
        out_shape=jax.ShapeDtypeStruct((b, (n_tiles - first) * TOKEN_TILE, D_MODEL), F32),
        scratch_shapes=[
            pltpu.VMEM((TOKEN_TILE + 2 * HALO, BRANCH_WIDTH), F32),
            pltpu.VMEM((TOKEN_TILE + 2 * HALO, BRANCH_WIDTH), F32),
            pltpu.VMEM((D_MODEL, N_BRANCH * D_MODEL), BF16),
            pltpu.VMEM((N_BRANCH, BRANCH_WIDTH, D_MODEL), BF16),
            pltpu.VMEM((D_MODEL, D_MODEL), BF16),
            pltpu.VMEM(WIDE_STAGE, F32),
            pltpu.VMEM(TALL_STAGE, F32),
            _DMA_SEMS,
        ],
        compiler_params=_params("arbitrary", "arbitrary"),
        name="merge",
    )(*streams, *ya, h, u, u, u, u, u, u, u, u, u, u, yf, modsel, conv_w, conv_b, w_pool,
      pool_scale, w_gate, b_gate, w_br, w_out)


def _ffn_kernel(*refs, final, layer):
    x_ref = refs[0]
    mod_refs = refs[1:1 + PAIR]
    (g_ref, w1_hbm, b1_ref, w2_hbm, b2_ref, fg_ref, o_ref,
     w1_ref, w2_ref, wide_stage, tall_stage, sem) = refs[1 + PAIR:]
    tiles = [(slice(k * TOKEN_TILE, (k + 1) * TOKEN_TILE), mod_refs[k]) for k in range(PAIR)]

    @pl.when(pl.program_id(0) == 0)
    def _():
        _load_rows_bf16(w1_hbm, layer, w1_ref, wide_stage, sem)
        _load_rows_bf16(w2_hbm, layer, w2_ref, tall_stage, sem)

    def normed(rows, mod_ref):
        shift = mod_ref[:, 3 * D_MODEL:4 * D_MODEL]
        scale = mod_ref[:, 4 * D_MODEL:5 * D_MODEL]
        return (_rms(x_ref[rows, :]) * g_ref[...] * (1.0 + scale) + shift).astype(BF16)

    hb = jnp.concatenate([normed(rows, mod_ref) for rows, mod_ref in tiles], axis=0)
    acc = jnp.zeros(x_ref.shape, F32)
    for c in range(D_FF // D_MODEL):
        cols = slice(c * D_MODEL, (c + 1) * D_MODEL)
        hid = jnp.square(jnp.maximum(_dot(hb, w1_ref[:, cols]) + b1_ref[:, cols], 0.0))
        acc = acc + _dot(hid.astype(BF16), w2_ref[cols, :])
    for rows, mod_ref in tiles:
        gate = mod_ref[:, 5 * D_MODEL:6 * D_MODEL]
        y = x_ref[rows, :] + gate * (acc[rows] + b2_ref[...])
        if final:
            y = _rms(y) * fg_ref[...]
        o_ref[rows, :] = y


def _ffn(xs, modsel, gain, w1, b1, w2, b2, final_g, layer, latent_only, final):
    b, rows, _ = xs.shape
    nt = rows // TOKEN_TILE
    first = 1 if latent_only else 0
    assert (b * nt) % PAIR == 0

    def mod(k):
        def index(s):
            tile = s * PAIR + k
            return (tile // nt, jnp.minimum(tile % nt + first, 1), 0, 0)
        return pl.BlockSpec((None, None, 1, N_MOD * D_MODEL), index)

    block = pl.BlockSpec((PAIR * TOKEN_TILE, D_MODEL), lambda s: (s, 0))
    out = pl.pallas_call(
        functools.partial(_ffn_kernel, final=final, layer=layer),
        grid=(b * nt // PAIR,),
        in_specs=[block] + [mod(k) for k in range(PAIR)] + [
            _const_spec((1, D_MODEL)),
            _HBM,
            _const_spec((1, D_FF)),
            _HBM,
            _const_spec((1, D_MODEL)),
            _const_spec((1, D_MODEL)),
        ],
        out_specs=block,
        out_shape=jax.ShapeDtypeStruct((b * rows, D_MODEL), F32),
        scratch_shapes=[
            pltpu.VMEM((D_MODEL, D_FF), BF16),
            pltpu.VMEM((D_FF, D_MODEL), BF16),
            pltpu.VMEM(WIDE_STAGE, F32),
            pltpu.VMEM(TALL_STAGE, F32),
            _DMA_SEMS,
        ],
        compiler_params=_params("arbitrary"),
        name="ffn",
    )(xs.reshape(b * rows, D_MODEL), *([modsel] * PAIR), gain, w1, b1, w2, b2, final_g)
    return out.reshape(b, rows, D_MODEL)


def _rope_tables(seq):
    lane = np.arange(LANES)
    d = lane % A_HEAD_DIM
    axis = d // AXIS_ROT
    upper = (d % AXIS_ROT) // (AXIS_ROT // 2)
    inv = ROPE_BASE ** (-(d % (AXIS_ROT // 2)) * 2.0 / AXIS_ROT)
    tok = np.arange(seq)
    pos = np.where(axis[None, :] == 0, (tok // GRID_W)[:, None], (tok % GRID_W)[:, None])
    ang = pos * inv[None, :]
    cos, sin = np.cos(ang), np.sin(ang)
    s_up = np.where(upper[None, :] == 0, -sin, 0.0)
    s_dn = np.where(upper[None, :] == 1, sin, 0.0)
    pad = lambda a, v: jnp.asarray(np.concatenate([np.full((CTX_LEN, LANES), v), a], axis=0), F32)
    return pad(cos, 1.0), pad(s_up, 0.0), pad(s_dn, 0.0)


def _dft_cos_sin(n, rows, cols):
    ang = (np.arange(rows)[:, None] * np.arange(cols)[None, :] % n) * (2.0 * math.pi / n)
    return jnp.asarray(np.cos(ang) * n ** -0.5, F32), jnp.asarray(np.sin(ang) * n ** -0.5, F32)


def kernel(x, c, ctx, c_ctx, w_mod, b_mod, norm1_g, w_in, lam_q1, lam_k1, lam_q2, lam_k2, subln_g, conv_w,
           conv_b, w_pool, pool_scale, w_gate, b_gate, w_br, w_out, norm2_g, w_ff1, b_ff1, w_ff2, b_ff2,
           final_g):
    batch, seq, d_model = x.shape
    depth = w_mod.shape[0]
    assert d_model == D_MODEL and ctx.shape[1] == CTX_LEN == TOKEN_TILE
    assert seq % (2 * REV_BLOCK) == 0 and batch + 1 <= COND_ROWS

    cond = jnp.concatenate([c, c_ctx[None, :], jnp.zeros((COND_ROWS - batch - 1, D_MODEL), F32)], axis=0)
    mods = _adaln(cond, w_mod, b_mod)

    rope = _rope_tables(seq)
    ch, sh = (a.astype(BF16) for a in _dft_cos_sin(seq, seq // 2, seq // 2))
    cc, sc = _dft_cos_sin(CTX_LEN, CTX_LEN, CTX_LEN)
    wc = jnp.concatenate([cc, -sc], axis=1).astype(BF16)
    cg, sg = _dft_cos_sin(FOURIER_GROUP, FOURIER_GROUP, FOURIER_GROUP)
    csg = jnp.concatenate([cg, sg], axis=1).astype(BF16)
    rev = jnp.asarray(np.eye(REV_BLOCK)[::-1], F32).astype(BF16)

    w_pool = w_pool.astype(BF16)

    streams = (ctx, x)
    row = lambda a: a.reshape(1, -1)
    for l in range(depth):
        last = l == depth - 1
        lam_init = 0.8 - 0.6 * math.exp(-0.3 * l)
        m = mods[l]
        modsel = jnp.stack([jnp.broadcast_to(m[batch], (batch, N_MOD * D_MODEL)), m[:batch]],
                           axis=1)[:, :, None, :]
        lamvec = jnp.pad(jnp.stack([lam_q1[l], lam_k1[l], lam_q2[l], lam_k2[l]]),
                         ((0, SUBLANES - 4), (0, LANES - A_HEAD_DIM)))

        if len(streams) == 2:
            u, h = _inproj_split(*streams, modsel, row(norm1_g[l]), w_in, l, rope)
        else:
            u, h = _inproj_pair(*streams, modsel, row(norm1_g[l]), w_in, l, rope)
        ya = _attention(u, lamvec, row(jnp.tile(subln_g[l], A_HEADS)), lam_init, ctx_tile=not last)
        yf = _fourier(u, ch, sh, rev, wc, csg, ctx_tile=not last)
        xs = _merge(streams, h, u, ya, yf, modsel, conv_w[l], row(conv_b[l]), w_pool, row(pool_scale[l]),
                    w_gate, row(b_gate[l]), w_br, w_out, l, ctx_tile=not last)
        xs = _ffn(xs, modsel, row(norm2_g[l]), w_ff1, row(b_ff1[l]), w_ff2, row(b_ff2[l]), row(final_g),
                  l, latent_only=last, final=last)
        streams = (xs,)
    return xs
```

```python
import functools
import math

import jax
import jax.numpy as jnp
import numpy as np
from jax import lax
from jax.experimental import pallas as pl
from jax.experimental.pallas import tpu as pltpu

F32 = jnp.float32
BF16 = jnp.bfloat16

D_MODEL = 1024
CTX_LEN = 256
GRID_W = 64
N_BRANCH = 4
BRANCH_WIDTH = 512
A_HEADS = 4
A_HEAD_DIM = 64
HEAD_COLS = 2 * A_HEAD_DIM
AXIS_ROT = A_HEAD_DIM // 2
ROPE_BASE = 10000.0
ATTN_SCALE = A_HEAD_DIM ** -0.5
CONV_WIDTH = 3
FOURIER_GROUPS = 4
FOURIER_GROUP = BRANCH_WIDTH // FOURIER_GROUPS
POOL_WINDOWS = (2, 4, 8, 16)
POOL_GROUP = BRANCH_WIDTH // len(POOL_WINDOWS)
N_MOD = 6
NORM_EPS = 1e-6
IN_WIDTH = 8 * BRANCH_WIDTH
D_FF = 4 * D_MODEL

TOKEN_TILE = 256
PAIR = 2
HALO = 16
COND_ROWS = 16
LANES = 128
SUBLANES = 8
REV_BLOCK = 256
LOG2_E = math.log2(math.e)
VMEM_LIMIT = 52 * 1024 * 1024


def _dot(a, b):
    return jnp.dot(a, b, preferred_element_type=F32)


def _rms(x):
    return x * lax.rsqrt(jnp.mean(x * x, axis=-1, keepdims=True) + NORM_EPS)


def _params(*sem):
    return pltpu.CompilerParams(dimension_semantics=sem, vmem_limit_bytes=VMEM_LIMIT)


def _const_spec(shape):
    zeros = (0,) * len(shape)
    return pl.BlockSpec(shape, lambda *_: zeros)


def _layer_spec(shape, layer):
    zeros = (0,) * len(shape)
    return pl.BlockSpec((None,) + shape, lambda *_: (layer,) + zeros)


def _mod_spec(first):
    return pl.BlockSpec((None, None, 1, N_MOD * D_MODEL), lambda bb, i: (bb, jnp.minimum(i + first, 1), 0, 0))


def _stream_specs(split, width, first):
    if split:
        return [pl.BlockSpec((None, TOKEN_TILE, width), lambda bb, i: (bb, 0, 0)),
                pl.BlockSpec((None, TOKEN_TILE, width), lambda bb, i: (bb, jnp.maximum(i + first - 1, 0), 0))]
    return [pl.BlockSpec((None, TOKEN_TILE, width), lambda bb, i: (bb, i + first, 0))]


def _stream_tile(refs, i):
    if len(refs) == 2:
        return jnp.where(i == 0, refs[0][...], refs[1][...])
    return refs[0][...]


STAGE_ROWS = 512
STREAM_SLOTS = 3
_HBM = pl.BlockSpec(memory_space=pl.ANY)


class _WeightStream:
    def __init__(self, chunks, stage, sem):
        self.slots = stage.shape[0]
        self.stage = stage
        self.dsts = [dst for _, dst in chunks]
        self.copies = [pltpu.make_async_copy(src, stage.at[k % self.slots], sem.at[k % self.slots])
                       for k, (src, _) in enumerate(chunks)]
        self.ready = 0
        for copy in self.copies[:self.slots]:
            copy.start()

    def need(self, n):
        while self.ready < n:
            k = self.ready
            self.copies[k].wait()
            self.dsts[k][...] = self.stage[k % self.slots].astype(BF16)
            if k + self.slots < len(self.copies):
                self.copies[k + self.slots].start()
            self.ready += 1


def _first_step_or_later(first_step, make_stream, step):
    pl.when(first_step)(lambda: step(make_stream().need))
    pl.when(jnp.logical_not(first_step))(lambda: step(lambda n: None))


def _adaln_kernel(cond_ref, w_ref, b_ref, o_ref):
    c = cond_ref[...]
    s = c * jax.nn.sigmoid(c)
    o_ref[...] = _dot(s.astype(BF16), w_ref[...].astype(BF16)) + b_ref[...]


def _adaln(cond, w_mod, b_mod):
    depth = w_mod.shape[0]
    return pl.pallas_call(
        _adaln_kernel,
        grid=(depth, N_MOD),
        in_specs=[
            pl.BlockSpec((COND_ROWS, D_MODEL), lambda l, j: (0, 0)),
            pl.BlockSpec((None, D_MODEL, D_MODEL), lambda l, j: (l, 0, j)),
            pl.BlockSpec((None, 1, D_MODEL), lambda l, j: (l, 0, j)),
        ],
        out_specs=pl.BlockSpec((None, COND_ROWS, D_MODEL), lambda l, j: (l, 0, j)),
        out_shape=jax.ShapeDtypeStruct((depth, COND_ROWS, N_MOD * D_MODEL), F32),
        compiler_params=_params("arbitrary", "arbitrary"),
        name="adaln",
    )(cond, w_mod, b_mod.reshape(depth, 1, N_MOD * D_MODEL))


def _inproj_body(x_tiles, mod_refs, rope_refs, g_ref, w_ref, u_ref, h_ref, need):
    hs = []
    for x, mod_ref in zip(x_tiles, mod_refs):
        shift = mod_ref[:, 0:D_MODEL]
        scale = mod_ref[:, D_MODEL:2 * D_MODEL]
        hs.append((_rms(x) * g_ref[...] * (1.0 + scale) + shift).astype(BF16))
    hb = jnp.concatenate(hs, axis=0)
    h_ref[...] = hb
    for j in range(IN_WIDTH // BRANCH_WIDTH):
        cols = slice(j * BRANCH_WIDTH, (j + 1) * BRANCH_WIDTH)
        need(j + 1)
        u = _dot(hb, w_ref[:, cols])
        if j < 2:
            tiles = []
            for k, (cos_ref, sup_ref, sdn_ref) in enumerate(rope_refs):
                parts = []
                for c in range(BRANCH_WIDTH // LANES):
                    uc = u[k * TOKEN_TILE:(k + 1) * TOKEN_TILE, c * LANES:(c + 1) * LANES]
                    up = pltpu.roll(uc, LANES - AXIS_ROT // 2, axis=1)
                    dn = pltpu.roll(uc, AXIS_ROT // 2, axis=1)
                    parts.append(uc * cos_ref[...] + up * sup_ref[...] + dn * sdn_ref[...])
                tiles.append(jnp.concatenate(parts, axis=1))
            u = jnp.concatenate(tiles, axis=0)
            if j == 0:
                u = u * (ATTN_SCALE * LOG2_E)
        u_ref[:, cols] = u.astype(BF16)


def _inproj_stream(w_hbm, w_ref, stage, sem, layer):
    blocks = [slice(j * BRANCH_WIDTH, (j + 1) * BRANCH_WIDTH) for j in range(IN_WIDTH // BRANCH_WIDTH)]
    return _WeightStream([(w_hbm.at[layer, :, cols], w_ref.at[:, cols]) for cols in blocks], stage, sem)


def _inproj_split_kernel(ctx_ref, x_ref, mod_ref, g_ref, w_hbm, cos_ref, sup_ref, sdn_ref, u_ref, h_ref,
                         w_ref, stage, sem, *, layer):
    def step(need):
        x = _stream_tile((ctx_ref, x_ref), pl.program_id(1))
        _inproj_body([x], [mod_ref], [(cos_ref, sup_ref, sdn_ref)], g_ref, w_ref, u_ref, h_ref, need)

    first_step = jnp.logical_and(pl.program_id(0) == 0, pl.program_id(1) == 0)
    _first_step_or_later(first_step, lambda: _inproj_stream(w_hbm, w_ref, stage, sem, layer), step)


def _inproj_pair_kernel(*refs, layer):
    x_ref = refs[0]
    mod_refs = refs[1:1 + PAIR]
    g_ref, w_hbm = refs[1 + PAIR:3 + PAIR]
    rope_refs = [refs[3 + PAIR + 3 * k:6 + PAIR + 3 * k] for k in range(PAIR)]
    u_ref, h_ref, w_ref, stage, sem = refs[3 + 4 * PAIR:]

    def step(need):
        x_tiles = [x_ref[k * TOKEN_TILE:(k + 1) * TOKEN_TILE, :] for k in range(PAIR)]
        _inproj_body(x_tiles, mod_refs, rope_refs, g_ref, w_ref, u_ref, h_ref, need)

    _first_step_or_later(pl.program_id(0) == 0, lambda: _inproj_stream(w_hbm, w_ref, stage, sem, layer), step)


_INPROJ_SCRATCH = [pltpu.VMEM((D_MODEL, IN_WIDTH), BF16),
                   pltpu.VMEM((STREAM_SLOTS, D_MODEL, BRANCH_WIDTH), F32),
                   pltpu.SemaphoreType.DMA((STREAM_SLOTS,))]


def _inproj_split(ctx, x, modsel, gain, w_in, layer, rope):
    b = x.shape[0]
    t = rope[0].shape[0]
    tile = lambda bb, i: (bb, i, 0)
    rope_spec = pl.BlockSpec((TOKEN_TILE, LANES), lambda bb, i: (i, 0))
    return pl.pallas_call(
        functools.partial(_inproj_split_kernel, layer=layer),
        grid=(b, t // TOKEN_TILE),
        in_specs=_stream_specs(True, D_MODEL, 0) + [
            _mod_spec(0),
            _const_spec((1, D_MODEL)),
            _HBM,
            rope_spec, rope_spec, rope_spec,
        ],
        out_specs=[
            pl.BlockSpec((None, TOKEN_TILE, IN_WIDTH), tile),
            pl.BlockSpec((None, TOKEN_TILE, D_MODEL), tile),
        ],
        out_shape=[
            jax.ShapeDtypeStruct((b, t, IN_WIDTH), BF16),
            jax.ShapeDtypeStruct((b, t, D_MODEL), BF16),
        ],
        scratch_shapes=_INPROJ_SCRATCH,
        compiler_params=_params("arbitrary", "arbitrary"),
        name="inproj",
    )(ctx, x, modsel, gain, w_in, *rope)


def _inproj_pair(xs, modsel, gain, w_in, layer, rope):
    b, t, _ = xs.shape
    nt = t // TOKEN_TILE
    assert (b * nt) % PAIR == 0

    def mod(k):
        def index(s):
            tile = s * PAIR + k
            return (tile // nt, jnp.minimum(tile % nt, 1), 0, 0)
        return pl.BlockSpec((None, None, 1, N_MOD * D_MODEL), index)

    def rope_specs(k):
        return [pl.BlockSpec((TOKEN_TILE, LANES), lambda s: ((s * PAIR + k) % nt, 0))] * 3

    rows = lambda width: pl.BlockSpec((PAIR * TOKEN_TILE, width), lambda s: (s, 0))
    u, h = pl.pallas_call(
        functools.partial(_inproj_pair_kernel, layer=layer),
        grid=(b * nt // PAIR,),
        in_specs=[rows(D_MODEL)] + [mod(k) for k in range(PAIR)] + [
            _const_spec((1, D_MODEL)),
            _HBM,
        ] + [spec for k in range(PAIR) for spec in rope_specs(k)],
        out_specs=[rows(IN_WIDTH), rows(D_MODEL)],
        out_shape=[
            jax.ShapeDtypeStruct((b * t, IN_WIDTH), BF16),
            jax.ShapeDtypeStruct((b * t, D_MODEL), BF16),
        ],
        scratch_shapes=_INPROJ_SCRATCH,
        compiler_params=_params("arbitrary"),
        name="inproj",
    )(xs.reshape(b * t, D_MODEL), *([modsel] * PAIR), gain, w_in, *(rope * PAIR))
    return u.reshape(b, t, IN_WIDTH), h.reshape(b, t, D_MODEL)


def _attn_kernel(*refs, lam_init, ctx_tile):
    q_refs = refs[:PAIR]
    k_ref, v_ref, lam_ref, g_ref = refs[PAIR:PAIR + 4]
    out_refs, vx_ref = refs[PAIR + 4:-1], refs[-1]
    lv = lam_ref[...]
    lam = (jnp.exp(jnp.sum(lv[0:1] * lv[1:2], axis=1, keepdims=True))
           - jnp.exp(jnp.sum(lv[2:3] * lv[3:4], axis=1, keepdims=True)) + lam_init)
    t = k_ref.shape[0]

    @pl.when(pl.program_id(1) == 0)
    def _():
        for h in range(A_HEADS):
            vx_ref[:, 2 * h * HEAD_COLS:(2 * h + 1) * HEAD_COLS] = v_ref[:, h * HEAD_COLS:(h + 1) * HEAD_COLS]
            vx_ref[:, (2 * h + 1) * HEAD_COLS:(2 * h + 2) * HEAD_COLS] = jnp.ones((t, HEAD_COLS), BF16)

    def run(q_refs, n_keys, o_ref):
        rows = len(q_refs) * TOKEN_TILE
        lane = lax.broadcasted_iota(jnp.int32, (rows, HEAD_COLS), 1)
        zero = jnp.zeros((rows, HEAD_COLS), BF16)

        def scores(chain):
            h, m = divmod(chain, 2)
            cols = slice(h * HEAD_COLS, (h + 1) * HEAD_COLS)
            q = jnp.concatenate([q_ref[:, cols] for q_ref in q_refs], axis=0)
            qm = jnp.where((lane >= A_HEAD_DIM) if m else (lane < A_HEAD_DIM), q, zero)
            return lax.dot_general(qm, k_ref[0:n_keys, cols], (((1,), (1,)), ((), ())),
                                   preferred_element_type=F32)

        n_chains = 2 * A_HEADS
        res = []
        s_next = scores(0)
        for chain in range(n_chains):
            s = s_next
            if chain + 1 < n_chains:
                s_next = scores(chain + 1)
            p = jnp.exp2(s - jnp.max(s, axis=-1, keepdims=True)).astype(BF16)
            h = chain // 2
            res.append(_dot(p, vx_ref[0:n_keys, 2 * h * HEAD_COLS:(2 * h + 2) * HEAD_COLS]))
        outs = []
        for h in range(A_HEADS):
            r1, r2 = res[2 * h], res[2 * h + 1]
            o = (r1[:, :HEAD_COLS] / r1[:, HEAD_COLS:] - lam * (r2[:, :HEAD_COLS] / r2[:, HEAD_COLS:]))
            outs.append(_rms(o))
        y = jnp.concatenate(outs, axis=1) * g_ref[...] * (1.0 - lam_init)
        o_ref[...] = y.astype(BF16)

    if ctx_tile:
        pl.when(pl.program_id(1) == 0)(lambda: run(q_refs[:1], CTX_LEN, out_refs[0]))
        pl.when(pl.program_id(1) > 0)(lambda: run(q_refs, t, out_refs[1]))
    else:
        run(q_refs, t, out_refs[0])


def _attention(u, lamvec, subln_g, lam_init, ctx_tile):
    b, t, _ = u.shape
    n_pairs = (t - CTX_LEN) // (PAIR * TOKEN_TILE)
    lead = 1 if ctx_tile else 0

    def q_spec(k):
        def index(bb, i):
            return (bb, jnp.maximum((i - lead) * PAIR + 1 + k, k), 0)
        return pl.BlockSpec((None, TOKEN_TILE, BRANCH_WIDTH), index)

    latent = pl.BlockSpec((None, PAIR * TOKEN_TILE, BRANCH_WIDTH), lambda bb, i: (bb, jnp.maximum(i - lead, 0), 0))
    latent_shape = jax.ShapeDtypeStruct((b, t - CTX_LEN, BRANCH_WIDTH), BF16)
    context = pl.BlockSpec((None, CTX_LEN, BRANCH_WIDTH), lambda bb, i: (bb, 0, 0))
    context_shape = jax.ShapeDtypeStruct((b, CTX_LEN, BRANCH_WIDTH), BF16)
    return pl.pallas_call(
        functools.partial(_attn_kernel, lam_init=lam_init, ctx_tile=ctx_tile),
        grid=(b, n_pairs + lead),
        in_specs=[q_spec(k) for k in range(PAIR)] + [
            pl.BlockSpec((None, t, BRANCH_WIDTH), lambda bb, i: (bb, 0, 1)),
            pl.BlockSpec((None, t, BRANCH_WIDTH), lambda bb, i: (bb, 0, 2)),
            _const_spec((SUBLANES, LANES)),
            _const_spec((1, BRANCH_WIDTH)),
        ],
        out_specs=[context, latent] if ctx_tile else [latent],
        out_shape=[context_shape, latent_shape] if ctx_tile else [latent_shape],
        scratch_shapes=[pltpu.VMEM((t, 2 * BRANCH_WIDTH), BF16)],
        compiler_params=_params("arbitrary", "arbitrary"),
        name="diff_attention",
    )(*([u] * PAIR), u, u, lamvec, subln_g)


def _fourier_kernel(f_ref, ch_ref, sh_ref, rev_ref, wc_ref, csg_ref, o_ref, buf, *, ctx_tile):
    t = f_ref.shape[0]
    seq = t - CTX_LEN
    m = seq // 2
    out0 = CTX_LEN if ctx_tile else 0

    def channel_dft(rows):
        zc, zs = [], []
        for g in range(FOURIER_GROUPS):
            zz = _dot(f_ref[rows, g * FOURIER_GROUP:(g + 1) * FOURIER_GROUP], csg_ref[...])
            zc.append(zz[:, :FOURIER_GROUP])
            zs.append(zz[:, FOURIER_GROUP:])
        return jnp.concatenate(zc, axis=1), jnp.concatenate(zs, axis=1)

    def reversed_shifted(a, row0):
        ab = a.astype(BF16)
        nb = m // REV_BLOCK
        for c in range(nb):
            buf[SUBLANES + c * REV_BLOCK:SUBLANES + (c + 1) * REV_BLOCK, :] = _dot(
                rev_ref[...], ab[(nb - 1 - c) * REV_BLOCK:(nb - c) * REV_BLOCK, :])
        buf[SUBLANES - 1:SUBLANES, :] = row0
        return buf[SUBLANES - 1:SUBLANES - 1 + m, :]

    if ctx_tile:
        zc, zs = channel_dft(slice(0, CTX_LEN))
        o_ref[0:CTX_LEN, :] = (_dot(wc_ref[:, :CTX_LEN], zc.astype(BF16))
                               + _dot(wc_ref[:, CTX_LEN:], zs.astype(BF16))).astype(BF16)

    zc, zs = channel_dft(slice(CTX_LEN, t))
    zero_row = jnp.zeros((1, BRANCH_WIDTH), F32)
    e = zc[:m] + reversed_shifted(zc[m:], zero_row)
    o = zs[:m] - reversed_shifted(zs[m:], zero_row)
    row = lax.broadcasted_iota(jnp.int32, (m, BRANCH_WIDTH), 0)
    sign = jnp.where((row & 1) == 0, 1.0, -1.0)
    nyq = zc[m:m + 1] * seq ** -0.5
    p = _dot(ch_ref[...], e.astype(BF16)) + sign * nyq
    q = _dot(sh_ref[...], o.astype(BF16))
    o_ref[out0:out0 + m, :] = (p - q).astype(BF16)
    y_mid = jnp.sum(sign * e, axis=0, keepdims=True) * seq ** -0.5 + nyq
    o_ref[out0 + m:out0 + seq, :] = reversed_shifted(p + q, y_mid).astype(BF16)


def _fourier(u, ch, sh, rev, wc, csg, ctx_tile):
    b, t, _ = u.shape
    seq = t - CTX_LEN
    out_t = t if ctx_tile else seq
    return pl.pallas_call(
        functools.partial(_fourier_kernel, ctx_tile=ctx_tile),
        grid=(b,),
        in_specs=[
            pl.BlockSpec((None, t, BRANCH_WIDTH), lambda bb: (bb, 0, 6)),
            _const_spec((seq // 2, seq // 2)),
            _const_spec((seq // 2, seq // 2)),
            _const_spec((REV_BLOCK, REV_BLOCK)),
            _const_spec((CTX_LEN, 2 * CTX_LEN)),
            _const_spec((FOURIER_GROUP, 2 * FOURIER_GROUP)),
        ],
        out_specs=pl.BlockSpec((None, out_t, BRANCH_WIDTH), lambda bb: (bb, 0, 0)),
        out_shape=jax.ShapeDtypeStruct((b, out_t, BRANCH_WIDTH), BF16),
        scratch_shapes=[pltpu.VMEM((seq // 2 + 2 * SUBLANES, BRANCH_WIDTH), F32)],
        compiler_params=_params("arbitrary"),
        name="fourier",
    )(u, ch, sh, rev, wc, csg)


def _merge_kernel(*refs, n_stream, first, n_tiles, layer):
    x_refs, ya_refs = refs[:n_stream], refs[n_stream:2 * n_stream]
    (h_ref, bg_ref, cg_ref, xs_ref, p_ref, cgp_ref, xsp_ref, pp_ref, cgn_ref, xsn_ref, pn_ref,
     yf_ref, mod_ref, convw_ref, convb_ref, wpool_ref, pscale_ref,
     wgate_hbm, bgate_ref, wbr_hbm, wout_hbm, o_ref,
     zbuf, pbuf, wgate_ref, wbr_ref, wout_ref, stage, sem) = refs[2 * n_stream:]
    rows_per_chunk = stage.shape[1]
    halves = D_MODEL // rows_per_chunk
    assert rows_per_chunk == BRANCH_WIDTH

    def square_chunks(w_hbm, w_ref, cols):
        return [(w_hbm.at[layer, r * rows_per_chunk:(r + 1) * rows_per_chunk, cols],
                 w_ref.at[r * rows_per_chunk:(r + 1) * rows_per_chunk, cols]) for r in range(halves)]

    def weight_chunks():
        chunks = []
        for n in range(N_BRANCH):
            chunks += square_chunks(wgate_hbm, wgate_ref, slice(n * D_MODEL, (n + 1) * D_MODEL))
            chunks.append((wbr_hbm.at[layer, n], wbr_ref.at[n]))
        return chunks + square_chunks(wout_hbm, wout_ref, slice(0, D_MODEL))

    i = pl.program_id(1) + first
    tm = TOKEN_TILE
    prev_ok = i >= 2
    next_ok = jnp.logical_and(i >= 1, i < n_tiles - 1)

    def step(need):
        z = cg_ref[...].astype(F32) * xs_ref[...].astype(F32)
        zp = cgp_ref[...].astype(F32) * xsp_ref[...].astype(F32)
        zn = cgn_ref[...].astype(F32) * xsn_ref[...].astype(F32)
        zbuf[0:HALO, :] = jnp.where(prev_ok, zp, 0.0)
        zbuf[HALO:HALO + tm, :] = z
        zbuf[HALO + tm:, :] = jnp.where(next_ok, zn, 0.0)
        conv = (zbuf[HALO - 1:HALO - 1 + tm, :] * convw_ref[0:1, :] + z * convw_ref[1:2, :]
                + zbuf[HALO + 1:HALO + 1 + tm, :] * convw_ref[2:3, :] + convb_ref[...])
        y_conv = (bg_ref[...].astype(F32) * conv).astype(BF16)

        p = p_ref[...].astype(F32)
        pbuf[0:HALO, :] = jnp.where(prev_ok, pp_ref[...].astype(F32), 0.0)
        pbuf[HALO:HALO + tm, :] = p
        pbuf[HALO + tm:, :] = jnp.where(next_ok, pn_ref[...].astype(F32), 0.0)
        pos = (lax.broadcasted_iota(jnp.int32, (tm, POOL_GROUP), 0)
               + jnp.where(i == 0, 0, (i - 1) * tm))
        seq_len = jnp.where(i == 0, CTX_LEN, (n_tiles - 1) * tm)
        pool_parts = []
        for g, w in enumerate(POOL_WINDOWS):
            cols = slice(g * POOL_GROUP, (g + 1) * POOL_GROUP)
            win = pbuf[HALO - w // 2:HALO - w // 2 + tm, cols]
            for d in range(1, w):
                win = win + pbuf[HALO - w // 2 + d:HALO - w // 2 + d + tm, cols]
            lo = jnp.maximum(pos - w // 2, 0)
            hi = jnp.minimum(pos - w // 2 + w, seq_len)
            pooled = (win / (hi - lo).astype(F32) - p[:, cols]).astype(BF16)
            pool_parts.append(_dot(pooled, wpool_ref[g]) * pscale_ref[:, cols])
        y_pool = jnp.concatenate(pool_parts, axis=1).astype(BF16)

        ys = (_stream_tile(ya_refs, i), y_conv, yf_ref[...], y_pool)
        hb = h_ref[...]
        acc = jnp.zeros((tm, D_MODEL), F32)
        for n in range(N_BRANCH):
            cols = slice(n * D_MODEL, (n + 1) * D_MODEL)
            need(n * (halves + 1) + halves)
            gate = jax.nn.sigmoid(_dot(hb, wgate_ref[:, cols]) + bgate_ref[:, cols])
            need((n + 1) * (halves + 1))
            acc = acc + gate * _dot(ys[n], wbr_ref[n])
        need(N_BRANCH * (halves + 1) + halves)
        out = _dot(acc.astype(BF16), wout_ref[...])
        g1 = mod_ref[:, 2 * D_MODEL:3 * D_MODEL]
        o_ref[...] = _stream_tile(x_refs, i) + g1 * out

    first_step = jnp.logical_and(pl.program_id(0) == 0, pl.program_id(1) == 0)
    _first_step_or_later(first_step, lambda: _WeightStream(weight_chunks(), stage, sem), step)


def _merge(streams, h, u, ya, yf, modsel, conv_w, conv_b, w_pool, pool_scale, w_gate, b_gate, w_br, w_out,
           layer, ctx_tile):
    assert len(streams) == len(ya)
    b, t, _ = u.shape
    n_tiles = t // TOKEN_TILE
    first = 0 if ctx_tile else 1
    hb = TOKEN_TILE // HALO
    col = lambda j: pl.BlockSpec((None, TOKEN_TILE, BRANCH_WIDTH), lambda bb, i: (bb, i + first, j))
    prev = lambda j: pl.BlockSpec(
        (None, HALO, BRANCH_WIDTH), lambda bb, i: (bb, jnp.maximum((i + first) * hb - 1, 0), j))
    nxt = lambda j: pl.BlockSpec(
        (None, HALO, BRANCH_WIDTH), lambda bb, i: (bb, jnp.minimum((i + first + 1) * hb, t // HALO - 1), j))
    branch = pl.BlockSpec((None, TOKEN_TILE, BRANCH_WIDTH), lambda bb, i: (bb, i, 0))
    return pl.pallas_call(
        functools.partial(_merge_kernel, n_stream=len(streams), first=first, n_tiles=n_tiles, layer=layer),
        grid=(b, n_tiles - first),
        in_specs=_stream_specs(len(streams) == 2, D_MODEL, first)
        + _stream_specs(len(ya) == 2, BRANCH_WIDTH, 0) + [
            pl.BlockSpec((None, TOKEN_TILE, D_MODEL), lambda bb, i: (bb, i + first, 0)),
            col(3), col(4), col(5), col(7),
            prev(4), prev(5), prev(7), nxt(4), nxt(5), nxt(7),
            branch,
            _mod_spec(first),
            _const_spec((CONV_WIDTH, BRANCH_WIDTH)),
            _const_spec((1, BRANCH_WIDTH)),
            _layer_spec((len(POOL_WINDOWS), POOL_GROUP, POOL_GROUP), layer),
            _const_spec((1, BRANCH_WIDTH)),
            _HBM,
            _const_spec((1, N_BRANCH * D_MODEL)),
            _HBM,
            _HBM,
        ],
        out_specs=pl.BlockSpec((None, TOKEN_TILE, D_MODEL), lambda bb, i: (bb, i, 0)),
        out_shape=jax.ShapeDtypeStruct((b, (n_tiles - first) * TOKEN_TILE, D_MODEL), F32),
        scratch_shapes=[
            pltpu.VMEM((TOKEN_TILE + 2 * HALO, BRANCH_WIDTH), F32),
            pltpu.VMEM((TOKEN_TILE + 2 * HALO, BRANCH_WIDTH), F32),
            pltpu.VMEM((D_MODEL, N_BRANCH * D_MODEL), BF16),
            pltpu.VMEM((N_BRANCH, BRANCH_WIDTH, D_MODEL), BF16),
            pltpu.VMEM((D_MODEL, D_MODEL), BF16),
            pltpu.VMEM((STREAM_SLOTS, STAGE_ROWS, D_MODEL), F32),
            pltpu.SemaphoreType.DMA((STREAM_SLOTS,)),
        ],
        compiler_params=_params("arbitrary", "arbitrary"),
        name="merge",
    )(*streams, *ya, h, u, u, u, u, u, u, u, u, u, u, yf, modsel, conv_w, conv_b, w_pool,
      pool_scale, w_gate, b_gate, w_br, w_out)


def _ffn_kernel(*refs, final, layer):
    x_ref = refs[0]
    mod_refs = refs[1:1 + PAIR]
    (g_ref, w1_hbm, b1_ref, w2_hbm, b2_ref, fg_ref, o_ref, w1_ref, w2_ref, stage, sem) = refs[1 + PAIR:]
    tiles = [(slice(k * TOKEN_TILE, (k + 1) * TOKEN_TILE), mod_refs[k]) for k in range(PAIR)]
    rows_per_chunk = stage.shape[1]
    halves = D_MODEL // rows_per_chunk

    def weight_chunks():
        chunks = []
        for c in range(D_FF // D_MODEL):
            cols = slice(c * D_MODEL, (c + 1) * D_MODEL)
            for r in range(halves):
                rows = slice(r * rows_per_chunk, (r + 1) * rows_per_chunk)
                chunks.append((w1_hbm.at[layer, rows, cols], w1_ref.at[rows, cols]))
            for r in range(halves):
                rows = slice(c * D_MODEL + r * rows_per_chunk, c * D_MODEL + (r + 1) * rows_per_chunk)
                chunks.append((w2_hbm.at[layer, rows, :], w2_ref.at[rows, :]))
        return chunks

    def normed(rows, mod_ref):
        shift = mod_ref[:, 3 * D_MODEL:4 * D_MODEL]
        scale = mod_ref[:, 4 * D_MODEL:5 * D_MODEL]
        return (_rms(x_ref[rows, :]) * g_ref[...] * (1.0 + scale) + shift).astype(BF16)

    def step(need):
        hb = jnp.concatenate([normed(rows, mod_ref) for rows, mod_ref in tiles], axis=0)
        acc = jnp.zeros(x_ref.shape, F32)
        for c in range(D_FF // D_MODEL):
            cols = slice(c * D_MODEL, (c + 1) * D_MODEL)
            need((2 * c + 1) * halves)
            hid = jnp.square(jnp.maximum(_dot(hb, w1_ref[:, cols]) + b1_ref[:, cols], 0.0))
            need((2 * c + 2) * halves)
            acc = acc + _dot(hid.astype(BF16), w2_ref[cols, :])
        for rows, mod_ref in tiles:
            gate = mod_ref[:, 5 * D_MODEL:6 * D_MODEL]
            y = x_ref[rows, :] + gate * (acc[rows] + b2_ref[...])
            if final:
                y = _rms(y) * fg_ref[...]
            o_ref[rows, :] = y

    _first_step_or_later(pl.program_id(0) == 0, lambda: _WeightStream(weight_chunks(), stage, sem), step)


def _ffn(xs, modsel, gain, w1, b1, w2, b2, final_g, layer, latent_only, final):
    b, rows, _ = xs.shape
    nt = rows // TOKEN_TILE
    first = 1 if latent_only else 0
    assert (b * nt) % PAIR == 0

    def mod(k):
        def index(s):
            tile = s * PAIR + k
            return (tile // nt, jnp.minimum(tile % nt + first, 1), 0, 0)
        return pl.BlockSpec((None, None, 1, N_MOD * D_MODEL), index)

    block = pl.BlockSpec((PAIR * TOKEN_TILE, D_MODEL), lambda s: (s, 0))
    out = pl.pallas_call(
        functools.partial(_ffn_kernel, final=final, layer=layer),
        grid=(b * nt // PAIR,),
        in_specs=[block] + [mod(k) for k in range(PAIR)] + [
            _const_spec((1, D_MODEL)),
            _HBM,
            _const_spec((1, D_FF)),
            _HBM,
            _const_spec((1, D_MODEL)),
            _const_spec((1, D_MODEL)),
        ],
        out_specs=block,
        out_shape=jax.ShapeDtypeStruct((b * rows, D_MODEL), F32),
        scratch_shapes=[
            pltpu.VMEM((D_MODEL, D_FF), BF16),
            pltpu.VMEM((D_FF, D_MODEL), BF16),
            pltpu.VMEM((STREAM_SLOTS, STAGE_ROWS, D_MODEL), F32),
            pltpu.SemaphoreType.DMA((STREAM_SLOTS,)),
        ],
        compiler_params=_params("arbitrary"),
        name="ffn",
    )(xs.reshape(b * rows, D_MODEL), *([modsel] * PAIR), gain, w1, b1, w2, b2, final_g)
    return out.reshape(b, rows, D_MODEL)


def _rope_tables(seq):
    lane = np.arange(LANES)
    d = lane % A_HEAD_DIM
    axis = d // AXIS_ROT
    upper = (d % AXIS_ROT) // (AXIS_ROT // 2)
    inv = ROPE_BASE ** (-(d % (AXIS_ROT // 2)) * 2.0 / AXIS_ROT)
    tok = np.arange(seq)
    pos = np.where(axis[None, :] == 0, (tok // GRID_W)[:, None], (tok % GRID_W)[:, None])
    ang = pos * inv[None, :]
    cos, sin = np.cos(ang), np.sin(ang)
    s_up = np.where(upper[None, :] == 0, -sin, 0.0)
    s_dn = np.where(upper[None, :] == 1, sin, 0.0)
    pad = lambda a, v: jnp.asarray(np.concatenate([np.full((CTX_LEN, LANES), v), a], axis=0), F32)
    return pad(cos, 1.0), pad(s_up, 0.0), pad(s_dn, 0.0)


def _dft_cos_sin(n, rows, cols):
    ang = (np.arange(rows)[:, None] * np.arange(cols)[None, :] % n) * (2.0 * math.pi / n)
    return jnp.asarray(np.cos(ang) * n ** -0.5, F32), jnp.asarray(np.sin(ang) * n ** -0.5, F32)


def kernel(x, c, ctx, c_ctx, w_mod, b_mod, norm1_g, w_in, lam_q1, lam_k1, lam_q2, lam_k2, subln_g, conv_w,
           conv_b, w_pool, pool_scale, w_gate, b_gate, w_br, w_out, norm2_g, w_ff1, b_ff1, w_ff2, b_ff2,
           final_g):
    batch, seq, d_model = x.shape
    depth = w_mod.shape[0]
    assert d_model == D_MODEL and ctx.shape[1] == CTX_LEN == TOKEN_TILE
    assert seq % (2 * REV_BLOCK) == 0 and batch + 1 <= COND_ROWS

    cond = jnp.concatenate([c, c_ctx[None, :], jnp.zeros((COND_ROWS - batch - 1, D_MODEL), F32)], axis=0)
    mods = _adaln(cond, w_mod, b_mod)

    rope = _rope_tables(seq)
    ch, sh = (a.astype(BF16) for a in _dft_cos_sin(seq, seq // 2, seq // 2))
    cc, sc = _dft_cos_sin(CTX_LEN, CTX_LEN, CTX_LEN)
    wc = jnp.concatenate([cc, -sc], axis=1).astype(BF16)
    cg, sg = _dft_cos_sin(FOURIER_GROUP, FOURIER_GROUP, FOURIER_GROUP)
    csg = jnp.concatenate([cg, sg], axis=1).astype(BF16)
    rev = jnp.asarray(np.eye(REV_BLOCK)[::-1], F32).astype(BF16)

    w_pool = w_pool.astype(BF16)

    streams = (ctx, x)
    row = lambda a: a.reshape(1, -1)
    for l in range(depth):
        last = l == depth - 1
        lam_init = 0.8 - 0.6 * math.exp(-0.3 * l)
        m = mods[l]
        modsel = jnp.stack([jnp.broadcast_to(m[batch], (batch, N_MOD * D_MODEL)), m[:batch]],
                           axis=1)[:, :, None, :]
        lamvec = jnp.pad(jnp.stack([lam_q1[l], lam_k1[l], lam_q2[l], lam_k2[l]]),
                         ((0, SUBLANES - 4), (0, LANES - A_HEAD_DIM)))

        if len(streams) == 2:
            u, h = _inproj_split(*streams, modsel, row(norm1_g[l]), w_in, l, rope)
        else:
            u, h = _inproj_pair(*streams, modsel, row(norm1_g[l]), w_in, l, rope)
        ya = _attention(u, lamvec, row(jnp.tile(subln_g[l], A_HEADS)), lam_init, ctx_tile=not last)
        yf = _fourier(u, ch, sh, rev, wc, csg, ctx_tile=not last)
        xs = _merge(streams, h, u, ya, yf, modsel, conv_w[l], row(conv_b[l]), w_pool, row(pool_scale[l]),
                    w_gate, row(b_gate[l]), w_br, w_out, l, ctx_tile=not last)
        xs = _ffn(xs, modsel, row(norm2_g[l]), w_ff1, row(b_ff1[l]), w_ff2, row(b_ff2[l]), row(final_g),
                  l, latent_only=last, final=last)
        streams = (xs,)
    return xs
```

```python
import functools
import math

import jax
import jax.numpy as jnp
import numpy as np
from jax import lax
from jax.experimental import pallas as pl
from jax.experimental.pallas import tpu as pltpu

F32 = jnp.float32
BF16 = jnp.bfloat16

D_MODEL = 1024
CTX_LEN = 256
GRID_W = 64
N_BRANCH = 4
BRANCH_WIDTH = 512
A_HEADS = 4
A_HEAD_DIM = 64
HEAD_COLS = 2 * A_HEAD_DIM
AXIS_ROT = A_HEAD_DIM // 2
ROPE_BASE = 10000.0
ATTN_SCALE = A_HEAD_DIM ** -0.5
CONV_WIDTH = 3
FOURIER_GROUPS = 4
FOURIER_GROUP = BRANCH_WIDTH // FOURIER_GROUPS
POOL_WINDOWS = (2, 4, 8, 16)
POOL_GROUP = BRANCH_WIDTH // len(POOL_WINDOWS)
N_MOD = 6
NORM_EPS = 1e-6
IN_WIDTH = 8 * BRANCH_WIDTH
D_FF = 4 * D_MODEL

TOKEN_TILE = 256
PAIR = 2
HALO = 16
COND_ROWS = 16
LANES = 128
SUBLANES = 8
REV_BLOCK = 256
LOG2_E = math.log2(math.e)
VMEM_LIMIT = 52 * 1024 * 1024


def _dot(a, b):
    return jnp.dot(a, b, preferred_element_type=F32)


def _rms(x):
    return x * lax.rsqrt(jnp.mean(x * x, axis=-1, keepdims=True) + NORM_EPS)


def _params(*sem):
    return pltpu.CompilerParams(dimension_semantics=sem, vmem_limit_bytes=VMEM_LIMIT)


def _const_spec(shape):
    zeros = (0,) * len(shape)
    return pl.BlockSpec(shape, lambda *_: zeros)


def _layer_spec(shape, layer):
    zeros = (0,) * len(shape)
    return pl.BlockSpec((None,) + shape, lambda *_: (layer,) + zeros)


def _mod_spec(first):
    return pl.BlockSpec((None, None, 1, N_MOD * D_MODEL), lambda bb, i: (bb, jnp.minimum(i + first, 1), 0, 0))


def _stream_specs(split, width, first):
    if split:
        return [pl.BlockSpec((None, TOKEN_TILE, width), lambda bb, i: (bb, 0, 0)),
                pl.BlockSpec((None, TOKEN_TILE, width), lambda bb, i: (bb, jnp.maximum(i + first - 1, 0), 0))]
    return [pl.BlockSpec((None, TOKEN_TILE, width), lambda bb, i: (bb, i + first, 0))]


def _stream_tile(refs, i):
    if len(refs) == 2:
        return jnp.where(i == 0, refs[0][...], refs[1][...])
    return refs[0][...]


STAGE_ROWS = 512
STREAM_SLOTS = 3
_HBM = pl.BlockSpec(memory_space=pl.ANY)


class _WeightStream:
    def __init__(self, chunks, stage, sem):
        self.slots = stage.shape[0]
        self.stage = stage
        self.dsts = [dst for _, dst in chunks]
        self.copies = [pltpu.make_async_copy(src, stage.at[k % self.slots], sem.at[k % self.slots])
                       for k, (src, _) in enumerate(chunks)]
        self.ready = 0
        for copy in self.copies[:self.slots]:
            copy.start()

    def need(self, n):
        while self.ready < n:
            k = self.ready
            self.copies[k].wait()
            self.dsts[k][...] = self.stage[k % self.slots].astype(BF16)
            if k + self.slots < len(self.copies):
                self.copies[k + self.slots].start()
            self.ready += 1


def _first_step_or_later(first_step, make_stream, step):
    pl.when(first_step)(lambda: step(make_stream().need))
    pl.when(jnp.logical_not(first_step))(lambda: step(lambda n: None))


def _adaln_kernel(cond_ref, w_ref, b_ref, o_ref):
    c = cond_ref[...]
    s = c * jax.nn.sigmoid(c)
    o_ref[...] = _dot(s.astype(BF16), w_ref[...].astype(BF16)) + b_ref[...]


def _adaln(cond, w_mod, b_mod):
    depth = w_mod.shape[0]
    return pl.pallas_call(
        _adaln_kernel,
        grid=(depth, N_MOD),
        in_specs=[
            pl.BlockSpec((COND_ROWS, D_MODEL), lambda l, j: (0, 0)),
            pl.BlockSpec((None, D_MODEL, D_MODEL), lambda l, j: (l, 0, j)),
            pl.BlockSpec((None, 1, D_MODEL), lambda l, j: (l, 0, j)),
        ],
        out_specs=pl.BlockSpec((None, COND_ROWS, D_MODEL), lambda l, j: (l, 0, j)),
        out_shape=jax.ShapeDtypeStruct((depth, COND_ROWS, N_MOD * D_MODEL), F32),
        compiler_params=_params("arbitrary", "arbitrary"),
        name="adaln",
    )(cond, w_mod, b_mod.reshape(depth, 1, N_MOD * D_MODEL))


def _inproj_body(x_tiles, mod_refs, rope_refs, g_ref, w_ref, u_ref, h_ref, need):
    hs = []
    for x, mod_ref in zip(x_tiles, mod_refs):
        shift = mod_ref[:, 0:D_MODEL]
        scale = mod_ref[:, D_MODEL:2 * D_MODEL]
        hs.append((_rms(x) * g_ref[...] * (1.0 + scale) + shift).astype(BF16))
    hb = jnp.concatenate(hs, axis=0)
    h_ref[...] = hb
    for j in range(IN_WIDTH // BRANCH_WIDTH):
        cols = slice(j * BRANCH_WIDTH, (j + 1) * BRANCH_WIDTH)
        need(j + 1)
        u = _dot(hb, w_ref[:, cols])
        if j < 2:
            tiles = []
            for k, (cos_ref, sup_ref, sdn_ref) in enumerate(rope_refs):
                parts = []
                for c in range(BRANCH_WIDTH // LANES):
                    uc = u[k * TOKEN_TILE:(k + 1) * TOKEN_TILE, c * LANES:(c + 1) * LANES]
                    up = pltpu.roll(uc, LANES - AXIS_ROT // 2, axis=1)
                    dn = pltpu.roll(uc, AXIS_ROT // 2, axis=1)
                    parts.append(uc * cos_ref[...] + up * sup_ref[...] + dn * sdn_ref[...])
                tiles.append(jnp.concatenate(parts, axis=1))
            u = jnp.concatenate(tiles, axis=0)
            if j == 0:
                u = u * (ATTN_SCALE * LOG2_E)
        u_ref[:, cols] = u.astype(BF16)


def _inproj_stream(w_hbm, w_ref, stage, sem, layer):
    blocks = [slice(j * BRANCH_WIDTH, (j + 1) * BRANCH_WIDTH) for j in range(IN_WIDTH // BRANCH_WIDTH)]
    return _WeightStream([(w_hbm.at[layer, :, cols], w_ref.at[:, cols]) for cols in blocks], stage, sem)


def _inproj_split_kernel(ctx_ref, x_ref, mod_ref, g_ref, w_hbm, cos_ref, sup_ref, sdn_ref, u_ref, h_ref,
                         w_ref, stage, sem, *, layer):
    def step(need):
        x = _stream_tile((ctx_ref, x_ref), pl.program_id(1))
        _inproj_body([x], [mod_ref], [(cos_ref, sup_ref, sdn_ref)], g_ref, w_ref, u_ref, h_ref, need)

    first_step = jnp.logical_and(pl.program_id(0) == 0, pl.program_id(1) == 0)
    _first_step_or_later(first_step, lambda: _inproj_stream(w_hbm, w_ref, stage, sem, layer), step)


def _inproj_pair_kernel(*refs, layer):
    x_ref = refs[0]
    mod_refs = refs[1:1 + PAIR]
    g_ref, w_hbm = refs[1 + PAIR:3 + PAIR]
    rope_refs = [refs[3 + PAIR + 3 * k:6 + PAIR + 3 * k] for k in range(PAIR)]
    u_ref, h_ref, w_ref, stage, sem = refs[3 + 4 * PAIR:]

    def step(need):
        x_tiles = [x_ref[k * TOKEN_TILE:(k + 1) * TOKEN_TILE, :] for k in range(PAIR)]
        _inproj_body(x_tiles, mod_refs, rope_refs, g_ref, w_ref, u_ref, h_ref, need)

    _first_step_or_later(pl.program_id(0) == 0, lambda: _inproj_stream(w_hbm, w_ref, stage, sem, layer), step)


_INPROJ_SCRATCH = [pltpu.VMEM((D_MODEL, IN_WIDTH), BF16),
                   pltpu.VMEM((STREAM_SLOTS, D_MODEL, BRANCH_WIDTH), F32),
                   pltpu.SemaphoreType.DMA((STREAM_SLOTS,))]


def _inproj_split(ctx, x, modsel, gain, w_in, layer, rope):
    b = x.shape[0]
    t = rope[0].shape[0]
    tile = lambda bb, i: (bb, i, 0)
    rope_spec = pl.BlockSpec((TOKEN_TILE, LANES), lambda bb, i: (i, 0))
    return pl.pallas_call(
        functools.partial(_inproj_split_kernel, layer=layer),
        grid=(b, t // TOKEN_TILE),
        in_specs=_stream_specs(True, D_MODEL, 0) + [
            _mod_spec(0),
            _const_spec((1, D_MODEL)),
            _HBM,
            rope_spec, rope_spec, rope_spec,
        ],
        out_specs=[
            pl.BlockSpec((None, TOKEN_TILE, IN_WIDTH), tile),
            pl.BlockSpec((None, TOKEN_TILE, D_MODEL), tile),
        ],
        out_shape=[
            jax.ShapeDtypeStruct((b, t, IN_WIDTH), BF16),
            jax.ShapeDtypeStruct((b, t, D_MODEL), BF16),
        ],
        scratch_shapes=_INPROJ_SCRATCH,
        compiler_params=_params("arbitrary", "arbitrary"),
        name="inproj",
    )(ctx, x, modsel, gain, w_in, *rope)


def _inproj_pair(xs, modsel, gain, w_in, layer, rope):
    b, t, _ = xs.shape
    nt = t // TOKEN_TILE
    assert (b * nt) % PAIR == 0

    def mod(k):
        def index(s):
            tile = s * PAIR + k
            return (tile // nt, jnp.minimum(tile % nt, 1), 0, 0)
        return pl.BlockSpec((None, None, 1, N_MOD * D_MODEL), index)

    def rope_specs(k):
        return [pl.BlockSpec((TOKEN_TILE, LANES), lambda s: ((s * PAIR + k) % nt, 0))] * 3

    rows = lambda width: pl.BlockSpec((PAIR * TOKEN_TILE, width), lambda s: (s, 0))
    u, h = pl.pallas_call(
        functools.partial(_inproj_pair_kernel, layer=layer),
        grid=(b * nt // PAIR,),
        in_specs=[rows(D_MODEL)] + [mod(k) for k in range(PAIR)] + [
            _const_spec((1, D_MODEL)),
            _HBM,
        ] + [spec for k in range(PAIR) for spec in rope_specs(k)],
        out_specs=[rows(IN_WIDTH), rows(D_MODEL)],
        out_shape=[
            jax.ShapeDtypeStruct((b * t, IN_WIDTH), BF16),
            jax.ShapeDtypeStruct((b * t, D_MODEL), BF16),
        ],
        scratch_shapes=_INPROJ_SCRATCH,
        compiler_params=_params("arbitrary"),
        name="inproj",
    )(xs.reshape(b * t, D_MODEL), *([modsel] * PAIR), gain, w_in, *(rope * PAIR))
    return u.reshape(b, t, IN_WIDTH), h.reshape(b, t, D_MODEL)


def _attn_kernel(*refs, lam_init, ctx_tile):
    q_refs = refs[:PAIR]
    k_ref, v_ref, lam_ref, g_ref = refs[PAIR:PAIR + 4]
    out_refs, vx_ref = refs[PAIR + 4:-1], refs[-1]
    lv = lam_ref[...]
    lam = (jnp.exp(jnp.sum(lv[0:1] * lv[1:2], axis=1, keepdims=True))
           - jnp.exp(jnp.sum(lv[2:3] * lv[3:4], axis=1, keepdims=True)) + lam_init)
    t = k_ref.shape[0]

    @pl.when(pl.program_id(1) == 0)
    def _():
        for h in range(A_HEADS):
            vx_ref[:, 2 * h * HEAD_COLS:(2 * h + 1) * HEAD_COLS] = v_ref[:, h * HEAD_COLS:(h + 1) * HEAD_COLS]
            vx_ref[:, (2 * h + 1) * HEAD_COLS:(2 * h + 2) * HEAD_COLS] = jnp.ones((t, HEAD_COLS), BF16)

    def run(q_refs, n_keys, o_ref):
        rows = len(q_refs) * TOKEN_TILE
        lane = lax.broadcasted_iota(jnp.int32, (rows, HEAD_COLS), 1)
        zero = jnp.zeros((rows, HEAD_COLS), BF16)

        def scores(chain):
            h, m = divmod(chain, 2)
            cols = slice(h * HEAD_COLS, (h + 1) * HEAD_COLS)
            q = jnp.concatenate([q_ref[:, cols] for q_ref in q_refs], axis=0)
            qm = jnp.where((lane >= A_HEAD_DIM) if m else (lane < A_HEAD_DIM), q, zero)
            return lax.dot_general(qm, k_ref[0:n_keys, cols], (((1,), (1,)), ((), ())),
                                   preferred_element_type=F32)

        units = [("joint", 2), ("wide", 0, 0), ("wide", 0, 1), ("joint", 3), ("wide", 1, 0), ("wide", 1, 1)]

        def issue(unit):
            return [scores(2 * unit[1] + m) for m in ((unit[2],) if unit[0] == "wide" else (0, 1))]

        def parts(s):
            p = jnp.exp2(s - jnp.max(s, axis=-1, keepdims=True))
            return p, jnp.sum(p, axis=-1, keepdims=True)

        wide, heads = {}, {}
        s_next = issue(units[0])
        for n, unit in enumerate(units):
            s = s_next
            if n + 1 < len(units):
                s_next = issue(units[n + 1])
            h = unit[1]
            if unit[0] == "wide":
                p = jnp.exp2(s[0] - jnp.max(s[0], axis=-1, keepdims=True)).astype(BF16)
                wide[h, unit[2]] = _dot(p, vx_ref[0:n_keys, 2 * h * HEAD_COLS:(2 * h + 2) * HEAD_COLS])
            else:
                (p1, l1), (p2, l2) = parts(s[0]), parts(s[1])
                a = (p1 * (1.0 / l1) - p2 * (lam / l2)).astype(BF16)
                heads[h] = _dot(a, v_ref[0:n_keys, h * HEAD_COLS:(h + 1) * HEAD_COLS])
        for (h, m), r in wide.items():
            if m == 0:
                r2 = wide[h, 1]
                heads[h] = r[:, :HEAD_COLS] / r[:, HEAD_COLS:] - lam * (r2[:, :HEAD_COLS] / r2[:, HEAD_COLS:])
        outs = [_rms(heads[h]) for h in range(A_HEADS)]
        y = jnp.concatenate(outs, axis=1) * g_ref[...] * (1.0 - lam_init)
        o_ref[...] = y.astype(BF16)

    if ctx_tile:
        pl.when(pl.program_id(1) == 0)(lambda: run(q_refs[:1], CTX_LEN, out_refs[0]))
        pl.when(pl.program_id(1) > 0)(lambda: run(q_refs, t, out_refs[1]))
    else:
        run(q_refs, t, out_refs[0])


def _attention(u, lamvec, subln_g, lam_init, ctx_tile):
    b, t, _ = u.shape
    n_pairs = (t - CTX_LEN) // (PAIR * TOKEN_TILE)
    lead = 1 if ctx_tile else 0

    def q_spec(k):
        def index(bb, i):
            return (bb, jnp.maximum((i - lead) * PAIR + 1 + k, k), 0)
        return pl.BlockSpec((None, TOKEN_TILE, BRANCH_WIDTH), index)

    latent = pl.BlockSpec((None, PAIR * TOKEN_TILE, BRANCH_WIDTH), lambda bb, i: (bb, jnp.maximum(i - lead, 0), 0))
    latent_shape = jax.ShapeDtypeStruct((b, t - CTX_LEN, BRANCH_WIDTH), BF16)
    context = pl.BlockSpec((None, CTX_LEN, BRANCH_WIDTH), lambda bb, i: (bb, 0, 0))
    context_shape = jax.ShapeDtypeStruct((b, CTX_LEN, BRANCH_WIDTH), BF16)
    return pl.pallas_call(
        functools.partial(_attn_kernel, lam_init=lam_init, ctx_tile=ctx_tile),
        grid=(b, n_pairs + lead),
        in_specs=[q_spec(k) for k in range(PAIR)] + [
            pl.BlockSpec((None, t, BRANCH_WIDTH), lambda bb, i: (bb, 0, 1)),
            pl.BlockSpec((None, t, BRANCH_WIDTH), lambda bb, i: (bb, 0, 2)),
            _const_spec((SUBLANES, LANES)),
            _const_spec((1, BRANCH_WIDTH)),
        ],
        out_specs=[context, latent] if ctx_tile else [latent],
        out_shape=[context_shape, latent_shape] if ctx_tile else [latent_shape],
        scratch_shapes=[pltpu.VMEM((t, 2 * BRANCH_WIDTH), BF16)],
        compiler_params=_params("arbitrary", "arbitrary"),
        name="diff_attention",
    )(*([u] * PAIR), u, u, lamvec, subln_g)


def _fourier_kernel(f_ref, ch_ref, sh_ref, rev_ref, wc_ref, csg_ref, o_ref, buf, *, ctx_tile):
    t = f_ref.shape[0]
    seq = t - CTX_LEN
    m = seq // 2
    out0 = CTX_LEN if ctx_tile else 0

    def channel_dft(rows):
        zc, zs = [], []
        for g in range(FOURIER_GROUPS):
            zz = _dot(f_ref[rows, g * FOURIER_GROUP:(g + 1) * FOURIER_GROUP], csg_ref[...])
            zc.append(zz[:, :FOURIER_GROUP])
            zs.append(zz[:, FOURIER_GROUP:])
        return jnp.concatenate(zc, axis=1), jnp.concatenate(zs, axis=1)

    def reversed_shifted(a, row0):
        ab = a.astype(BF16)
        nb = m // REV_BLOCK
        for c in range(nb):
            buf[SUBLANES + c * REV_BLOCK:SUBLANES + (c + 1) * REV_BLOCK, :] = _dot(
                rev_ref[...], ab[(nb - 1 - c) * REV_BLOCK:(nb - c) * REV_BLOCK, :])
        buf[SUBLANES - 1:SUBLANES, :] = row0
        return buf[SUBLANES - 1:SUBLANES - 1 + m, :]

    if ctx_tile:
        zc, zs = channel_dft(slice(0, CTX_LEN))
        o_ref[0:CTX_LEN, :] = (_dot(wc_ref[:, :CTX_LEN], zc.astype(BF16))
                               + _dot(wc_ref[:, CTX_LEN:], zs.astype(BF16))).astype(BF16)

    zc, zs = channel_dft(slice(CTX_LEN, t))
    zero_row = jnp.zeros((1, BRANCH_WIDTH), F32)
    e = zc[:m] + reversed_shifted(zc[m:], zero_row)
    o = zs[:m] - reversed_shifted(zs[m:], zero_row)
    row = lax.broadcasted_iota(jnp.int32, (m, BRANCH_WIDTH), 0)
    sign = jnp.where((row & 1) == 0, 1.0, -1.0)
    nyq = zc[m:m + 1] * seq ** -0.5
    p = _dot(ch_ref[...], e.astype(BF16)) + sign * nyq
    q = _dot(sh_ref[...], o.astype(BF16))
    o_ref[out0:out0 + m, :] = (p - q).astype(BF16)
    y_mid = jnp.sum(sign * e, axis=0, keepdims=True) * seq ** -0.5 + nyq
    o_ref[out0 + m:out0 + seq, :] = reversed_shifted(p + q, y_mid).astype(BF16)


def _fourier(u, ch, sh, rev, wc, csg, ctx_tile):
    b, t, _ = u.shape
    seq = t - CTX_LEN
    out_t = t if ctx_tile else seq
    return pl.pallas_call(
        functools.partial(_fourier_kernel, ctx_tile=ctx_tile),
        grid=(b,),
        in_specs=[
            pl.BlockSpec((None, t, BRANCH_WIDTH), lambda bb: (bb, 0, 6)),
            _const_spec((seq // 2, seq // 2)),
            _const_spec((seq // 2, seq // 2)),
            _const_spec((REV_BLOCK, REV_BLOCK)),
            _const_spec((CTX_LEN, 2 * CTX_LEN)),
            _const_spec((FOURIER_GROUP, 2 * FOURIER_GROUP)),
        ],
        out_specs=pl.BlockSpec((None, out_t, BRANCH_WIDTH), lambda bb: (bb, 0, 0)),
        out_shape=jax.ShapeDtypeStruct((b, out_t, BRANCH_WIDTH), BF16),
        scratch_shapes=[pltpu.VMEM((seq // 2 + 2 * SUBLANES, BRANCH_WIDTH), F32)],
        compiler_params=_params("arbitrary"),
        name="fourier",
    )(u, ch, sh, rev, wc, csg)


def _merge_kernel(*refs, n_stream, first, n_tiles, layer):
    x_refs, ya_refs = refs[:n_stream], refs[n_stream:2 * n_stream]
    (h_ref, bg_ref, cg_ref, xs_ref, p_ref, cgp_ref, xsp_ref, pp_ref, cgn_ref, xsn_ref, pn_ref,
     yf_ref, mod_ref, convw_ref, convb_ref, wpool_ref, pscale_ref,
     wgate_hbm, bgate_ref, wbr_hbm, wout_hbm, o_ref,
     zbuf, pbuf, wgate_ref, wbr_ref, wout_ref, stage, sem) = refs[2 * n_stream:]
    rows_per_chunk = stage.shape[1]
    halves = D_MODEL // rows_per_chunk
    assert rows_per_chunk == BRANCH_WIDTH

    def square_chunks(w_hbm, w_ref, cols):
        return [(w_hbm.at[layer, r * rows_per_chunk:(r + 1) * rows_per_chunk, cols],
                 w_ref.at[r * rows_per_chunk:(r + 1) * rows_per_chunk, cols]) for r in range(halves)]

    def weight_chunks():
        chunks = []
        for n in range(N_BRANCH):
            chunks += square_chunks(wgate_hbm, wgate_ref, slice(n * D_MODEL, (n + 1) * D_MODEL))
            chunks.append((wbr_hbm.at[layer, n], wbr_ref.at[n]))
        return chunks + square_chunks(wout_hbm, wout_ref, slice(0, D_MODEL))

    i = pl.program_id(1) + first
    tm = TOKEN_TILE
    prev_ok = i >= 2
    next_ok = jnp.logical_and(i >= 1, i < n_tiles - 1)

    def step(need):
        z = cg_ref[...].astype(F32) * xs_ref[...].astype(F32)
        zp = cgp_ref[...].astype(F32) * xsp_ref[...].astype(F32)
        zn = cgn_ref[...].astype(F32) * xsn_ref[...].astype(F32)
        zbuf[0:HALO, :] = jnp.where(prev_ok, zp, 0.0)
        zbuf[HALO:HALO + tm, :] = z
        zbuf[HALO + tm:, :] = jnp.where(next_ok, zn, 0.0)
        conv = (zbuf[HALO - 1:HALO - 1 + tm, :] * convw_ref[0:1, :] + z * convw_ref[1:2, :]
                + zbuf[HALO + 1:HALO + 1 + tm, :] * convw_ref[2:3, :] + convb_ref[...])
        y_conv = (bg_ref[...].astype(F32) * conv).astype(BF16)

        p = p_ref[...].astype(F32)
        pbuf[0:HALO, :] = jnp.where(prev_ok, pp_ref[...].astype(F32), 0.0)
        pbuf[HALO:HALO + tm, :] = p
        pbuf[HALO + tm:, :] = jnp.where(next_ok, pn_ref[...].astype(F32), 0.0)
        pos = (lax.broadcasted_iota(jnp.int32, (tm, POOL_GROUP), 0)
               + jnp.where(i == 0, 0, (i - 1) * tm))
        seq_len = jnp.where(i == 0, CTX_LEN, (n_tiles - 1) * tm)
        pool_parts = []
        for g, w in enumerate(POOL_WINDOWS):
            cols = slice(g * POOL_GROUP, (g + 1) * POOL_GROUP)
            win = pbuf[HALO - w // 2:HALO - w // 2 + tm, cols]
            for d in range(1, w):
                win = win + pbuf[HALO - w // 2 + d:HALO - w // 2 + d + tm, cols]
            lo = jnp.maximum(pos - w // 2, 0)
            hi = jnp.minimum(pos - w // 2 + w, seq_len)
            pooled = (win / (hi - lo).astype(F32) - p[:, cols]).astype(BF16)
            pool_parts.append(_dot(pooled, wpool_ref[g]) * pscale_ref[:, cols])
        y_pool = jnp.concatenate(pool_parts, axis=1).astype(BF16)

        ys = (_stream_tile(ya_refs, i), y_conv, yf_ref[...], y_pool)
        hb = h_ref[...]
        acc = jnp.zeros((tm, D_MODEL), F32)
        for n in range(N_BRANCH):
            cols = slice(n * D_MODEL, (n + 1) * D_MODEL)
            need(n * (halves + 1) + halves)
            gate = jax.nn.sigmoid(_dot(hb, wgate_ref[:, cols]) + bgate_ref[:, cols])
            need((n + 1) * (halves + 1))
            acc = acc + gate * _dot(ys[n], wbr_ref[n])
        need(N_BRANCH * (halves + 1) + halves)
        out = _dot(acc.astype(BF16), wout_ref[...])
        g1 = mod_ref[:, 2 * D_MODEL:3 * D_MODEL]
        o_ref[...] = _stream_tile(x_refs, i) + g1 * out

    first_step = jnp.logical_and(pl.program_id(0) == 0, pl.program_id(1) == 0)
    _first_step_or_later(first_step, lambda: _WeightStream(weight_chunks(), stage, sem), step)


def _merge(streams, h, u, ya, yf, modsel, conv_w, conv_b, w_pool, pool_scale, w_gate, b_gate, w_br, w_out,
           layer, ctx_tile):
    assert len(streams) == len(ya)
    b, t, _ = u.shape
    n_tiles = t // TOKEN_TILE
    first = 0 if ctx_tile else 1
    hb = TOKEN_TILE // HALO
    col = lambda j: pl.BlockSpec((None, TOKEN_TILE, BRANCH_WIDTH), lambda bb, i: (bb, i + first, j))
    prev = lambda j: pl.BlockSpec(
        (None, HALO, BRANCH_WIDTH), lambda bb, i: (bb, jnp.maximum((i + first) * hb - 1, 0), j))
    nxt = lambda j: pl.BlockSpec(
        (None, HALO, BRANCH_WIDTH), lambda bb, i: (bb, jnp.minimum((i + first + 1) * hb, t // HALO - 1), j))
    branch = pl.BlockSpec((None, TOKEN_TILE, BRANCH_WIDTH), lambda bb, i: (bb, i, 0))
    return pl.pallas_call(
        functools.partial(_merge_kernel, n_stream=len(streams), first=first, n_tiles=n_tiles, layer=layer),
        grid=(b, n_tiles - first),
        in_specs=_stream_specs(len(streams) == 2, D_MODEL, first)
        + _stream_specs(len(ya) == 2, BRANCH_WIDTH, 0) + [
            pl.BlockSpec((None, TOKEN_TILE, D_MODEL), lambda bb, i: (bb, i + first, 0)),
            col(3), col(4), col(5), col(7),
            prev(4), prev(5), prev(7), nxt(4), nxt(5), nxt(7),
            branch,
            _mod_spec(first),
            _const_spec((CONV_WIDTH, BRANCH_WIDTH)),
            _const_spec((1, BRANCH_WIDTH)),
            _layer_spec((len(POOL_WINDOWS), POOL_GROUP, POOL_GROUP), layer),
            _const_spec((1, BRANCH_WIDTH)),
            _HBM,
            _const_spec((1, N_BRANCH * D_MODEL)),
            _HBM,
            _HBM,
        ],
        out_specs=pl.BlockSpec((None, TOKEN_TILE, D_MODEL), lambda bb, i: (bb, i, 0)),
        out_shape=jax.ShapeDtypeStruct((b, (n_tiles - first) * TOKEN_TILE, D_MODEL), F32),
        scratch_shapes=[
            pltpu.VMEM((TOKEN_TILE + 2 * HALO, BRANCH_WIDTH), F32),
            pltpu.VMEM((TOKEN_TILE + 2 * HALO, BRANCH_WIDTH), F32),
            pltpu.VMEM((D_MODEL, N_BRANCH * D_MODEL), BF16),
            pltpu.VMEM((N_BRANCH, BRANCH_WIDTH, D_MODEL), BF16),
            pltpu.VMEM((D_MODEL, D_MODEL), BF16),
            pltpu.VMEM((STREAM_SLOTS, STAGE_ROWS, D_MODEL), F32),
            pltpu.SemaphoreType.DMA((STREAM_SLOTS,)),
        ],
        compiler_params=_params("arbitrary", "arbitrary"),
        name="merge",
    )(*streams, *ya, h, u, u, u, u, u, u, u, u, u, u, yf, modsel, conv_w, conv_b, w_pool,
      pool_scale, w_gate, b_gate, w_br, w_out)


def _ffn_kernel(*refs, final, layer):
    x_ref = refs[0]
    mod_refs = refs[1:1 + PAIR]
    (g_ref, w1_hbm, b1_ref, w2_hbm, b2_ref, fg_ref, o_ref, w1_ref, w2_ref, stage, sem) = refs[1 + PAIR:]
    tiles = [(slice(k * TOKEN_TILE, (k + 1) * TOKEN_TILE), mod_refs[k]) for k in range(PAIR)]
    rows_per_chunk = stage.shape[1]
    halves = D_MODEL // rows_per_chunk

    def weight_chunks():
        chunks = []
        for c in range(D_FF // D_MODEL):
            cols = slice(c * D_MODEL, (c + 1) * D_MODEL)
            for r in range(halves):
                rows = slice(r * rows_per_chunk, (r + 1) * rows_per_chunk)
                chunks.append((w1_hbm.at[layer, rows, cols], w1_ref.at[rows, cols]))
            for r in range(halves):
                rows = slice(c * D_MODEL + r * rows_per_chunk, c * D_MODEL + (r + 1) * rows_per_chunk)
                chunks.append((w2_hbm.at[layer, rows, :], w2_ref.at[rows, :]))
        return chunks

    def normed(rows, mod_ref):
        shift = mod_ref[:, 3 * D_MODEL:4 * D_MODEL]
        scale = mod_ref[:, 4 * D_MODEL:5 * D_MODEL]
        return (_rms(x_ref[rows, :]) * g_ref[...] * (1.0 + scale) + shift).astype(BF16)

    def step(need):
        hb = jnp.concatenate([normed(rows, mod_ref) for rows, mod_ref in tiles], axis=0)
        acc = jnp.zeros(x_ref.shape, F32)
        for c in range(D_FF // D_MODEL):
            cols = slice(c * D_MODEL, (c + 1) * D_MODEL)
            need((2 * c + 1) * halves)
            hid = jnp.square(jnp.maximum(_dot(hb, w1_ref[:, cols]) + b1_ref[:, cols], 0.0))
            need((2 * c + 2) * halves)
            acc = acc + _dot(hid.astype(BF16), w2_ref[cols, :])
        for rows, mod_ref in tiles:
            gate = mod_ref[:, 5 * D_MODEL:6 * D_MODEL]
            y = x_ref[rows, :] + gate * (acc[rows] + b2_ref[...])
            if final:
                y = _rms(y) * fg_ref[...]
            o_ref[rows, :] = y

    _first_step_or_later(pl.program_id(0) == 0, lambda: _WeightStream(weight_chunks(), stage, sem), step)


def _ffn(xs, modsel, gain, w1, b1, w2, b2, final_g, layer, latent_only, final):
    b, rows, _ = xs.shape
    nt = rows // TOKEN_TILE
    first = 1 if latent_only else 0
    assert (b * nt) % PAIR == 0

    def mod(k):
        def index(s):
            tile = s * PAIR + k
            return (tile // nt, jnp.minimum(tile % nt + first, 1), 0, 0)
        return pl.BlockSpec((None, None, 1, N_MOD * D_MODEL), index)

    block = pl.BlockSpec((PAIR * TOKEN_TILE, D_MODEL), lambda s: (s, 0))
    out = pl.pallas_call(
        functools.partial(_ffn_kernel, final=final, layer=layer),
        grid=(b * nt // PAIR,),
        in_specs=[block] + [mod(k) for k in range(PAIR)] + [
            _const_spec((1, D_MODEL)),
            _HBM,
            _const_spec((1, D_FF)),
            _HBM,
            _const_spec((1, D_MODEL)),
            _const_spec((1, D_MODEL)),
        ],
        out_specs=block,
        out_shape=jax.ShapeDtypeStruct((b * rows, D_MODEL), F32),
        scratch_shapes=[
            pltpu.VMEM((D_MODEL, D_FF), BF16),
            pltpu.VMEM((D_FF, D_MODEL), BF16),
            pltpu.VMEM((STREAM_SLOTS, STAGE_ROWS, D_MODEL), F32),
            pltpu.SemaphoreType.DMA((STREAM_SLOTS,)),
        ],
        compiler_params=_params("arbitrary"),
        name="ffn",
    )(xs.reshape(b * rows, D_MODEL), *([modsel] * PAIR), gain, w1, b1, w2, b2, final_g)
    return out.reshape(b, rows, D_MODEL)


def _rope_tables(seq):
    lane = np.arange(LANES)
    d = lane % A_HEAD_DIM
    axis = d // AXIS_ROT
    upper = (d % AXIS_ROT) // (AXIS_ROT // 2)
    inv = ROPE_BASE ** (-(d % (AXIS_ROT // 2)) * 2.0 / AXIS_ROT)
    tok = np.arange(seq)
    pos = np.where(axis[None, :] == 0, (tok // GRID_W)[:, None], (tok % GRID_W)[:, None])
    ang = pos * inv[None, :]
    cos, sin = np.cos(ang), np.sin(ang)
    s_up = np.where(upper[None, :] == 0, -sin, 0.0)
    s_dn = np.where(upper[None, :] == 1, sin, 0.0)
    pad = lambda a, v: jnp.asarray(np.concatenate([np.full((CTX_LEN, LANES), v), a], axis=0), F32)
    return pad(cos, 1.0), pad(s_up, 0.0), pad(s_dn, 0.0)


def _dft_cos_sin(n, rows, cols):
    ang = (np.arange(rows)[:, None] * np.arange(cols)[None, :] % n) * (2.0 * math.pi / n)
    return jnp.asarray(np.cos(ang) * n ** -0.5, F32), jnp.asarray(np.sin(ang) * n ** -0.5, F32)


def kernel(x, c, ctx, c_ctx, w_mod, b_mod, norm1_g, w_in, lam_q1, lam_k1, lam_q2, lam_k2, subln_g, conv_w,
           conv_b, w_pool, pool_scale, w_gate, b_gate, w_br, w_out, norm2_g, w_ff1, b_ff1, w_ff2, b_ff2,
           final_g):
    batch, seq, d_model = x.shape
    depth = w_mod.shape[0]
    assert d_model == D_MODEL and ctx.shape[1] == CTX_LEN == TOKEN_TILE
    assert seq % (2 * REV_BLOCK) == 0 and batch + 1 <= COND_ROWS

    cond = jnp.concatenate([c, c_ctx[None, :], jnp.zeros((COND_ROWS - batch - 1, D_MODEL), F32)], axis=0)
    mods = _adaln(cond, w_mod, b_mod)

    rope = _rope_tables(seq)
    ch, sh = (a.astype(BF16) for a in _dft_cos_sin(seq, seq // 2, seq // 2))
    cc, sc = _dft_cos_sin(CTX_LEN, CTX_LEN, CTX_LEN)
    wc = jnp.concatenate([cc, -sc], axis=1).astype(BF16)
    cg, sg = _dft_cos_sin(FOURIER_GROUP, FOURIER_GROUP, FOURIER_GROUP)
    csg = jnp.concatenate([cg, sg], axis=1).astype(BF16)
    rev = jnp.asarray(np.eye(REV_BLOCK)[::-1], F32).astype(BF16)

    w_pool = w_pool.astype(BF16)

    streams = (ctx, x)
    row = lambda a: a.reshape(1, -1)
    for l in range(depth):
        last = l == depth - 1
        lam_init = 0.8 - 0.6 * math.exp(-0.3 * l)
        m = mods[l]
        modsel = jnp.stack([jnp.broadcast_to(m[batch], (batch, N_MOD * D_MODEL)), m[:batch]],
                           axis=1)[:, :, None, :]
        lamvec = jnp.pad(jnp.stack([lam_q1[l], lam_k1[l], lam_q2[l], lam_k2[l]]),
                         ((0, SUBLANES - 4), (0, LANES - A_HEAD_DIM)))

        if len(streams) == 2:
            u, h = _inproj_split(*streams, modsel, row(norm1_g[l]), w_in, l, rope)
        else:
            u, h = _inproj_pair(*streams, modsel, row(norm1_g[l]), w_in, l, rope)
        ya = _attention(u, lamvec, row(jnp.tile(subln_g[l], A_HEADS)), lam_init, ctx_tile=not last)
        yf = _fourier(u, ch, sh, rev, wc, csg, ctx_tile=not last)
        xs = _merge(streams, h, u, ya, yf, modsel, conv_w[l], row(conv_b[l]), w_pool, row(pool_scale[l]),
                    w_gate, row(b_gate[l]), w_br, w_out, l, ctx_tile=not last)
        xs = _ffn(xs, modsel, row(norm2_g[l]), w_ff1, row(b_ff1[l]), w_ff2, row(b_ff2[l]), row(final_g),
                  l, latent_only=last, final=last)
        streams = (xs,)
    return xs
```

```python
import functools
import math

import jax
import jax.numpy as jnp
import numpy as np
from jax import lax
from jax.experimental import pallas as pl
from jax.experimental.pallas import tpu as pltpu

F32 = jnp.float32
BF16 = jnp.bfloat16

D_MODEL = 1024
CTX_LEN = 256
GRID_W = 64
N_BRANCH = 4
BRANCH_WIDTH = 512
A_HEADS = 4
A_HEAD_DIM = 64
HEAD_COLS = 2 * A_HEAD_DIM
AXIS_ROT = A_HEAD_DIM // 2
ROPE_BASE = 10000.0
ATTN_SCALE = A_HEAD_DIM ** -0.5
CONV_WIDTH = 3
FOURIER_GROUPS = 4
FOURIER_GROUP = BRANCH_WIDTH // FOURIER_GROUPS
POOL_WINDOWS = (2, 4, 8, 16)
POOL_GROUP = BRANCH_WIDTH // len(POOL_WINDOWS)
N_MOD = 6
NORM_EPS = 1e-6
IN_WIDTH = 8 * BRANCH_WIDTH
COL_Q, COL_K, COL_V, COL_BG, COL_CG, COL_XS, COL_F, COL_P = range(IN_WIDTH // BRANCH_WIDTH)
D_FF = 4 * D_MODEL

TOKEN_TILE = 256
PAIR = 2
WIDE_HEADS, JOINT_HEADS = (0, 1), (2, 3)
HALO = 16
COND_ROWS = 16
LANES = 128
SUBLANES = 8
REV_BLOCK = 256
LOG2_E = math.log2(math.e)
VMEM_LIMIT = 52 * 1024 * 1024


def _dot(a, b):
    return jnp.dot(a, b, preferred_element_type=F32)


def _rms(x):
    return x * lax.rsqrt(jnp.mean(x * x, axis=-1, keepdims=True) + NORM_EPS)


def _params(*sem):
    return pltpu.CompilerParams(dimension_semantics=sem, vmem_limit_bytes=VMEM_LIMIT)


def _const_spec(shape):
    zeros = (0,) * len(shape)
    return pl.BlockSpec(shape, lambda *_: zeros)


def _layer_spec(shape, layer):
    zeros = (0,) * len(shape)
    return pl.BlockSpec((None,) + shape, lambda *_: (layer,) + zeros)


def _mod_spec(first):
    return pl.BlockSpec((None, None, 1, N_MOD * D_MODEL), lambda bb, i: (bb, jnp.minimum(i + first, 1), 0, 0))


def _stream_specs(split, width, first):
    if split:
        return [pl.BlockSpec((None, TOKEN_TILE, width), lambda bb, i: (bb, 0, 0)),
                pl.BlockSpec((None, TOKEN_TILE, width), lambda bb, i: (bb, jnp.maximum(i + first - 1, 0), 0))]
    return [pl.BlockSpec((None, TOKEN_TILE, width), lambda bb, i: (bb, i + first, 0))]


def _stream_tile(refs, i):
    if len(refs) == 2:
        return jnp.where(i == 0, refs[0][...], refs[1][...])
    return refs[0][...]


STAGE_ROWS = 512
STREAM_SLOTS = 3
_HBM = pl.BlockSpec(memory_space=pl.ANY)


class _WeightStream:
    def __init__(self, chunks, stage, sem):
        self.slots = stage.shape[0]
        self.stage = stage
        self.dsts = [dst for _, dst in chunks]
        self.copies = [pltpu.make_async_copy(src, stage.at[k % self.slots], sem.at[k % self.slots])
                       for k, (src, _) in enumerate(chunks)]
        self.ready = 0
        for copy in self.copies[:self.slots]:
            copy.start()

    def need(self, n):
        while self.ready < n:
            k = self.ready
            self.copies[k].wait()
            self.dsts[k][...] = self.stage[k % self.slots].astype(BF16)
            if k + self.slots < len(self.copies):
                self.copies[k + self.slots].start()
            self.ready += 1


def _first_step_or_later(first_step, make_stream, step):
    pl.when(first_step)(lambda: step(make_stream().need))
    pl.when(jnp.logical_not(first_step))(lambda: step(lambda n: None))


def _adaln_kernel(cond_ref, w_ref, b_ref, o_ref):
    c = cond_ref[...]
    s = c * jax.nn.sigmoid(c)
    o_ref[...] = _dot(s.astype(BF16), w_ref[...].astype(BF16)) + b_ref[...]


def _adaln(cond, w_mod, b_mod):
    depth = w_mod.shape[0]
    return pl.pallas_call(
        _adaln_kernel,
        grid=(depth, N_MOD),
        in_specs=[
            pl.BlockSpec((COND_ROWS, D_MODEL), lambda l, j: (0, 0)),
            pl.BlockSpec((None, D_MODEL, D_MODEL), lambda l, j: (l, 0, j)),
            pl.BlockSpec((None, 1, D_MODEL), lambda l, j: (l, 0, j)),
        ],
        out_specs=pl.BlockSpec((None, COND_ROWS, D_MODEL), lambda l, j: (l, 0, j)),
        out_shape=jax.ShapeDtypeStruct((depth, COND_ROWS, N_MOD * D_MODEL), F32),
        compiler_params=_params("arbitrary", "arbitrary"),
        name="adaln",
    )(cond, w_mod, b_mod.reshape(depth, 1, N_MOD * D_MODEL))


def _inproj_body(x_tiles, mod_refs, rope_refs, g_ref, w_ref, u_ref, h_ref, need):
    hs = []
    for x, mod_ref in zip(x_tiles, mod_refs):
        shift = mod_ref[:, 0:D_MODEL]
        scale = mod_ref[:, D_MODEL:2 * D_MODEL]
        hs.append((_rms(x) * g_ref[...] * (1.0 + scale) + shift).astype(BF16))
    hb = jnp.concatenate(hs, axis=0)
    h_ref[...] = hb
    for j in range(IN_WIDTH // BRANCH_WIDTH):
        cols = slice(j * BRANCH_WIDTH, (j + 1) * BRANCH_WIDTH)
        need(j + 1)
        u = _dot(hb, w_ref[:, cols])
        if j in (COL_Q, COL_K):
            tiles = []
            for k, (cos_ref, sup_ref, sdn_ref) in enumerate(rope_refs):
                parts = []
                for c in range(BRANCH_WIDTH // LANES):
                    uc = u[k * TOKEN_TILE:(k + 1) * TOKEN_TILE, c * LANES:(c + 1) * LANES]
                    up = pltpu.roll(uc, LANES - AXIS_ROT // 2, axis=1)
                    dn = pltpu.roll(uc, AXIS_ROT // 2, axis=1)
                    parts.append(uc * cos_ref[...] + up * sup_ref[...] + dn * sdn_ref[...])
                tiles.append(jnp.concatenate(parts, axis=1))
            u = jnp.concatenate(tiles, axis=0)
            if j == COL_Q:
                u = u * (ATTN_SCALE * LOG2_E)
        u_ref[:, cols] = u.astype(BF16)


def _inproj_stream(w_hbm, w_ref, stage, sem, layer):
    blocks = [slice(j * BRANCH_WIDTH, (j + 1) * BRANCH_WIDTH) for j in range(IN_WIDTH // BRANCH_WIDTH)]
    return _WeightStream([(w_hbm.at[layer, :, cols], w_ref.at[:, cols]) for cols in blocks], stage, sem)


def _inproj_split_kernel(ctx_ref, x_ref, mod_ref, g_ref, w_hbm, cos_ref, sup_ref, sdn_ref, u_ref, h_ref,
                         w_ref, stage, sem, *, layer):
    def step(need):
        x = _stream_tile((ctx_ref, x_ref), pl.program_id(1))
        _inproj_body([x], [mod_ref], [(cos_ref, sup_ref, sdn_ref)], g_ref, w_ref, u_ref, h_ref, need)

    first_step = jnp.logical_and(pl.program_id(0) == 0, pl.program_id(1) == 0)
    _first_step_or_later(first_step, lambda: _inproj_stream(w_hbm, w_ref, stage, sem, layer), step)


def _inproj_pair_kernel(*refs, layer):
    x_ref = refs[0]
    mod_refs = refs[1:1 + PAIR]
    g_ref, w_hbm = refs[1 + PAIR:3 + PAIR]
    rope_refs = [refs[3 + PAIR + 3 * k:6 + PAIR + 3 * k] for k in range(PAIR)]
    u_ref, h_ref, w_ref, stage, sem = refs[3 + 4 * PAIR:]

    def step(need):
        x_tiles = [x_ref[k * TOKEN_TILE:(k + 1) * TOKEN_TILE, :] for k in range(PAIR)]
        _inproj_body(x_tiles, mod_refs, rope_refs, g_ref, w_ref, u_ref, h_ref, need)

    _first_step_or_later(pl.program_id(0) == 0, lambda: _inproj_stream(w_hbm, w_ref, stage, sem, layer), step)


_INPROJ_SCRATCH = [pltpu.VMEM((D_MODEL, IN_WIDTH), BF16),
                   pltpu.VMEM((STREAM_SLOTS, D_MODEL, BRANCH_WIDTH), F32),
                   pltpu.SemaphoreType.DMA((STREAM_SLOTS,))]


def _inproj_split(ctx, x, modsel, gain, w_in, layer, rope):
    b = x.shape[0]
    t = rope[0].shape[0]
    tile = lambda bb, i: (bb, i, 0)
    rope_spec = pl.BlockSpec((TOKEN_TILE, LANES), lambda bb, i: (i, 0))
    return pl.pallas_call(
        functools.partial(_inproj_split_kernel, layer=layer),
        grid=(b, t // TOKEN_TILE),
        in_specs=_stream_specs(True, D_MODEL, 0) + [
            _mod_spec(0),
            _const_spec((1, D_MODEL)),
            _HBM,
            rope_spec, rope_spec, rope_spec,
        ],
        out_specs=[
            pl.BlockSpec((None, TOKEN_TILE, IN_WIDTH), tile),
            pl.BlockSpec((None, TOKEN_TILE, D_MODEL), tile),
        ],
        out_shape=[
            jax.ShapeDtypeStruct((b, t, IN_WIDTH), BF16),
            jax.ShapeDtypeStruct((b, t, D_MODEL), BF16),
        ],
        scratch_shapes=_INPROJ_SCRATCH,
        compiler_params=_params("arbitrary", "arbitrary"),
        name="inproj",
    )(ctx, x, modsel, gain, w_in, *rope)


def _inproj_pair(xs, modsel, gain, w_in, layer, rope):
    b, t, _ = xs.shape
    nt = t // TOKEN_TILE
    assert (b * nt) % PAIR == 0

    def mod(k):
        def index(s):
            tile = s * PAIR + k
            return (tile // nt, jnp.minimum(tile % nt, 1), 0, 0)
        return pl.BlockSpec((None, None, 1, N_MOD * D_MODEL), index)

    def rope_specs(k):
        return [pl.BlockSpec((TOKEN_TILE, LANES), lambda s: ((s * PAIR + k) % nt, 0))] * 3

    rows = lambda width: pl.BlockSpec((PAIR * TOKEN_TILE, width), lambda s: (s, 0))
    u, h = pl.pallas_call(
        functools.partial(_inproj_pair_kernel, layer=layer),
        grid=(b * nt // PAIR,),
        in_specs=[rows(D_MODEL)] + [mod(k) for k in range(PAIR)] + [
            _const_spec((1, D_MODEL)),
            _HBM,
        ] + [spec for k in range(PAIR) for spec in rope_specs(k)],
        out_specs=[rows(IN_WIDTH), rows(D_MODEL)],
        out_shape=[
            jax.ShapeDtypeStruct((b * t, IN_WIDTH), BF16),
            jax.ShapeDtypeStruct((b * t, D_MODEL), BF16),
        ],
        scratch_shapes=_INPROJ_SCRATCH,
        compiler_params=_params("arbitrary"),
        name="inproj",
    )(xs.reshape(b * t, D_MODEL), *([modsel] * PAIR), gain, w_in, *(rope * PAIR))
    return u.reshape(b, t, IN_WIDTH), h.reshape(b, t, D_MODEL)


def _attn_kernel(*refs, lam_init, ctx_tile):
    q_refs = refs[:PAIR]
    k_ref, v_ref, lam_ref, g_ref = refs[PAIR:PAIR + 4]
    out_refs, vx_ref = refs[PAIR + 4:-1], refs[-1]
    lv = lam_ref[...]
    lam = (jnp.exp(jnp.sum(lv[0:1] * lv[1:2], axis=1, keepdims=True))
           - jnp.exp(jnp.sum(lv[2:3] * lv[3:4], axis=1, keepdims=True)) + lam_init)
    t = k_ref.shape[0]

    @pl.when(pl.program_id(1) == 0)
    def _():
        for w, h in enumerate(WIDE_HEADS):
            vx_ref[:, 2 * w * HEAD_COLS:(2 * w + 1) * HEAD_COLS] = v_ref[:, h * HEAD_COLS:(h + 1) * HEAD_COLS]
            vx_ref[:, (2 * w + 1) * HEAD_COLS:(2 * w + 2) * HEAD_COLS] = jnp.ones((t, HEAD_COLS), BF16)

    def run(q_refs, n_keys, o_ref):
        rows = len(q_refs) * TOKEN_TILE
        lane = lax.broadcasted_iota(jnp.int32, (rows, HEAD_COLS), 1)
        zero = jnp.zeros((rows, HEAD_COLS), BF16)

        def scores(chain):
            h, m = divmod(chain, 2)
            cols = slice(h * HEAD_COLS, (h + 1) * HEAD_COLS)
            q = jnp.concatenate([q_ref[:, cols] for q_ref in q_refs], axis=0)
            qm = jnp.where((lane >= A_HEAD_DIM) if m else (lane < A_HEAD_DIM), q, zero)
            return lax.dot_general(qm, k_ref[0:n_keys, cols], (((1,), (1,)), ((), ())),
                                   preferred_element_type=F32)

        units = []
        for hw, hj in zip(WIDE_HEADS, JOINT_HEADS):
            units += [("joint", hj), ("wide", hw, 0), ("wide", hw, 1)]

        def issue(unit):
            return [scores(2 * unit[1] + m) for m in ((unit[2],) if unit[0] == "wide" else (0, 1))]

        def parts(s):
            p = jnp.exp2(s - jnp.max(s, axis=-1, keepdims=True))
            return p, jnp.sum(p, axis=-1, keepdims=True)

        wide, heads = {}, {}
        s_next = issue(units[0])
        for n, unit in enumerate(units):
            s = s_next
            if n + 1 < len(units):
                s_next = issue(units[n + 1])
            h = unit[1]
            if unit[0] == "wide":
                p = jnp.exp2(s[0] - jnp.max(s[0], axis=-1, keepdims=True)).astype(BF16)
                w = WIDE_HEADS.index(h)
                wide[h, unit[2]] = _dot(p, vx_ref[0:n_keys, 2 * w * HEAD_COLS:(2 * w + 2) * HEAD_COLS])
            else:
                (p1, l1), (p2, l2) = parts(s[0]), parts(s[1])
                a = (p1 * (1.0 / l1) - p2 * (lam / l2)).astype(BF16)
                heads[h] = _dot(a, v_ref[0:n_keys, h * HEAD_COLS:(h + 1) * HEAD_COLS])
        for (h, m), r in wide.items():
            if m == 0:
                r2 = wide[h, 1]
                heads[h] = r[:, :HEAD_COLS] / r[:, HEAD_COLS:] - lam * (r2[:, :HEAD_COLS] / r2[:, HEAD_COLS:])
        outs = [_rms(heads[h]) for h in range(A_HEADS)]
        y = jnp.concatenate(outs, axis=1) * g_ref[...] * (1.0 - lam_init)
        o_ref[...] = y.astype(BF16)

    if ctx_tile:
        pl.when(pl.program_id(1) == 0)(lambda: run(q_refs[:1], CTX_LEN, out_refs[0]))
        pl.when(pl.program_id(1) > 0)(lambda: run(q_refs, t, out_refs[1]))
    else:
        run(q_refs, t, out_refs[0])


def _attention(u, lamvec, subln_g, lam_init, ctx_tile):
    b, t, _ = u.shape
    n_pairs = (t - CTX_LEN) // (PAIR * TOKEN_TILE)
    lead = 1 if ctx_tile else 0

    def q_spec(k):
        def index(bb, i):
            return (bb, jnp.maximum((i - lead) * PAIR + 1 + k, k), 0)
        return pl.BlockSpec((None, TOKEN_TILE, BRANCH_WIDTH), index)

    latent = pl.BlockSpec((None, PAIR * TOKEN_TILE, BRANCH_WIDTH), lambda bb, i: (bb, jnp.maximum(i - lead, 0), 0))
    latent_shape = jax.ShapeDtypeStruct((b, t - CTX_LEN, BRANCH_WIDTH), BF16)
    context = pl.BlockSpec((None, CTX_LEN, BRANCH_WIDTH), lambda bb, i: (bb, 0, 0))
    context_shape = jax.ShapeDtypeStruct((b, CTX_LEN, BRANCH_WIDTH), BF16)
    return pl.pallas_call(
        functools.partial(_attn_kernel, lam_init=lam_init, ctx_tile=ctx_tile),
        grid=(b, n_pairs + lead),
        in_specs=[q_spec(k) for k in range(PAIR)] + [
            pl.BlockSpec((None, t, BRANCH_WIDTH), lambda bb, i: (bb, 0, COL_K)),
            pl.BlockSpec((None, t, BRANCH_WIDTH), lambda bb, i: (bb, 0, COL_V)),
            _const_spec((SUBLANES, LANES)),
            _const_spec((1, BRANCH_WIDTH)),
        ],
        out_specs=[context, latent] if ctx_tile else [latent],
        out_shape=[context_shape, latent_shape] if ctx_tile else [latent_shape],
        scratch_shapes=[pltpu.VMEM((t, 2 * HEAD_COLS * len(WIDE_HEADS)), BF16)],
        compiler_params=_params("arbitrary", "arbitrary"),
        name="diff_attention",
    )(*([u] * PAIR), u, u, lamvec, subln_g)


def _fourier_kernel(f_ref, ch_ref, sh_ref, rev_ref, wc_ref, csg_ref, o_ref, buf, *, ctx_tile):
    t = f_ref.shape[0]
    seq = t - CTX_LEN
    m = seq // 2
    out0 = CTX_LEN if ctx_tile else 0

    def channel_dft(rows):
        zc, zs = [], []
        for g in range(FOURIER_GROUPS):
            zz = _dot(f_ref[rows, g * FOURIER_GROUP:(g + 1) * FOURIER_GROUP], csg_ref[...])
            zc.append(zz[:, :FOURIER_GROUP])
            zs.append(zz[:, FOURIER_GROUP:])
        return jnp.concatenate(zc, axis=1), jnp.concatenate(zs, axis=1)

    def reversed_shifted(a, row0):
        ab = a.astype(BF16)
        nb = m // REV_BLOCK
        for c in range(nb):
            buf[SUBLANES + c * REV_BLOCK:SUBLANES + (c + 1) * REV_BLOCK, :] = _dot(
                rev_ref[...], ab[(nb - 1 - c) * REV_BLOCK:(nb - c) * REV_BLOCK, :])
        buf[SUBLANES - 1:SUBLANES, :] = row0
        return buf[SUBLANES - 1:SUBLANES - 1 + m, :]

    if ctx_tile:
        zc, zs = channel_dft(slice(0, CTX_LEN))
        o_ref[0:CTX_LEN, :] = (_dot(wc_ref[:, :CTX_LEN], zc.astype(BF16))
                               + _dot(wc_ref[:, CTX_LEN:], zs.astype(BF16))).astype(BF16)

    zc, zs = channel_dft(slice(CTX_LEN, t))
    zero_row = jnp.zeros((1, BRANCH_WIDTH), F32)
    e = zc[:m] + reversed_shifted(zc[m:], zero_row)
    o = zs[:m] - reversed_shifted(zs[m:], zero_row)
    row = lax.broadcasted_iota(jnp.int32, (m, BRANCH_WIDTH), 0)
    sign = jnp.where((row & 1) == 0, 1.0, -1.0)
    nyq = zc[m:m + 1] * seq ** -0.5
    p = _dot(ch_ref[...], e.astype(BF16)) + sign * nyq
    q = _dot(sh_ref[...], o.astype(BF16))
    o_ref[out0:out0 + m, :] = (p - q).astype(BF16)
    y_mid = jnp.sum(sign * e, axis=0, keepdims=True) * seq ** -0.5 + nyq
    o_ref[out0 + m:out0 + seq, :] = reversed_shifted(p + q, y_mid).astype(BF16)


def _fourier(u, ch, sh, rev, wc, csg, ctx_tile):
    b, t, _ = u.shape
    seq = t - CTX_LEN
    out_t = t if ctx_tile else seq
    return pl.pallas_call(
        functools.partial(_fourier_kernel, ctx_tile=ctx_tile),
        grid=(b,),
        in_specs=[
            pl.BlockSpec((None, t, BRANCH_WIDTH), lambda bb: (bb, 0, COL_F)),
            _const_spec((seq // 2, seq // 2)),
            _const_spec((seq // 2, seq // 2)),
            _const_spec((REV_BLOCK, REV_BLOCK)),
            _const_spec((CTX_LEN, 2 * CTX_LEN)),
            _const_spec((FOURIER_GROUP, 2 * FOURIER_GROUP)),
        ],
        out_specs=pl.BlockSpec((None, out_t, BRANCH_WIDTH), lambda bb: (bb, 0, 0)),
        out_shape=jax.ShapeDtypeStruct((b, out_t, BRANCH_WIDTH), BF16),
        scratch_shapes=[pltpu.VMEM((seq // 2 + 2 * SUBLANES, BRANCH_WIDTH), F32)],
        compiler_params=_params("arbitrary"),
        name="fourier",
    )(u, ch, sh, rev, wc, csg)


def _merge_kernel(*refs, n_stream, first, n_tiles, layer):
    x_refs, ya_refs = refs[:n_stream], refs[n_stream:2 * n_stream]
    (h_ref, bg_ref, cg_ref, xs_ref, p_ref, cgp_ref, xsp_ref, pp_ref, cgn_ref, xsn_ref, pn_ref,
     yf_ref, mod_ref, convw_ref, convb_ref, wpool_ref, pscale_ref,
     wgate_hbm, bgate_ref, wbr_hbm, wout_hbm, o_ref,
     zbuf, pbuf, wgate_ref, wbr_ref, wout_ref, stage, sem) = refs[2 * n_stream:]
    rows_per_chunk = stage.shape[1]
    halves = D_MODEL // rows_per_chunk
    assert rows_per_chunk == BRANCH_WIDTH

    def square_chunks(w_hbm, w_ref, cols):
        return [(w_hbm.at[layer, r * rows_per_chunk:(r + 1) * rows_per_chunk, cols],
                 w_ref.at[r * rows_per_chunk:(r + 1) * rows_per_chunk, cols]) for r in range(halves)]

    def weight_chunks():
        chunks = []
        for n in range(N_BRANCH):
            chunks += square_chunks(wgate_hbm, wgate_ref, slice(n * D_MODEL, (n + 1) * D_MODEL))
            chunks.append((wbr_hbm.at[layer, n], wbr_ref.at[n]))
        return chunks + square_chunks(wout_hbm, wout_ref, slice(0, D_MODEL))

    i = pl.program_id(1) + first
    tm = TOKEN_TILE
    prev_ok = i >= 2
    next_ok = jnp.logical_and(i >= 1, i < n_tiles - 1)

    def step(need):
        z = cg_ref[...].astype(F32) * xs_ref[...].astype(F32)
        zp = cgp_ref[...].astype(F32) * xsp_ref[...].astype(F32)
        zn = cgn_ref[...].astype(F32) * xsn_ref[...].astype(F32)
        zbuf[0:HALO, :] = jnp.where(prev_ok, zp, 0.0)
        zbuf[HALO:HALO + tm, :] = z
        zbuf[HALO + tm:, :] = jnp.where(next_ok, zn, 0.0)
        conv = (zbuf[HALO - 1:HALO - 1 + tm, :] * convw_ref[0:1, :] + z * convw_ref[1:2, :]
                + zbuf[HALO + 1:HALO + 1 + tm, :] * convw_ref[2:3, :] + convb_ref[...])
        y_conv = (bg_ref[...].astype(F32) * conv).astype(BF16)

        p = p_ref[...].astype(F32)
        pbuf[0:HALO, :] = jnp.where(prev_ok, pp_ref[...].astype(F32), 0.0)
        pbuf[HALO:HALO + tm, :] = p
        pbuf[HALO + tm:, :] = jnp.where(next_ok, pn_ref[...].astype(F32), 0.0)
        pos = (lax.broadcasted_iota(jnp.int32, (tm, POOL_GROUP), 0)
               + jnp.where(i == 0, 0, (i - 1) * tm))
        seq_len = jnp.where(i == 0, CTX_LEN, (n_tiles - 1) * tm)
        pool_parts = []
        for g, w in enumerate(POOL_WINDOWS):
            cols = slice(g * POOL_GROUP, (g + 1) * POOL_GROUP)
            win = pbuf[HALO - w // 2:HALO - w // 2 + tm, cols]
            for d in range(1, w):
                win = win + pbuf[HALO - w // 2 + d:HALO - w // 2 + d + tm, cols]
            lo = jnp.maximum(pos - w // 2, 0)
            hi = jnp.minimum(pos - w // 2 + w, seq_len)
            pooled = (win / (hi - lo).astype(F32) - p[:, cols]).astype(BF16)
            pool_parts.append(_dot(pooled, wpool_ref[g]) * pscale_ref[:, cols])
        y_pool = jnp.concatenate(pool_parts, axis=1).astype(BF16)

        ys = (_stream_tile(ya_refs, i), y_conv, yf_ref[...], y_pool)
        hb = h_ref[...]
        acc = jnp.zeros((tm, D_MODEL), F32)
        for n in range(N_BRANCH):
            cols = slice(n * D_MODEL, (n + 1) * D_MODEL)
            need(n * (halves + 1) + halves)
            gate = jax.nn.sigmoid(_dot(hb, wgate_ref[:, cols]) + bgate_ref[:, cols])
            need((n + 1) * (halves + 1))
            acc = acc + gate * _dot(ys[n], wbr_ref[n])
        need(N_BRANCH * (halves + 1) + halves)
        out = _dot(acc.astype(BF16), wout_ref[...])
        g1 = mod_ref[:, 2 * D_MODEL:3 * D_MODEL]
        o_ref[...] = _stream_tile(x_refs, i) + g1 * out

    first_step = jnp.logical_and(pl.program_id(0) == 0, pl.program_id(1) == 0)
    _first_step_or_later(first_step, lambda: _WeightStream(weight_chunks(), stage, sem), step)


def _merge(streams, h, u, ya, yf, modsel, conv_w, conv_b, w_pool, pool_scale, w_gate, b_gate, w_br, w_out,
           layer, ctx_tile):
    assert len(streams) == len(ya)
    b, t, _ = u.shape
    n_tiles = t // TOKEN_TILE
    first = 0 if ctx_tile else 1
    hb = TOKEN_TILE // HALO
    col = lambda j: pl.BlockSpec((None, TOKEN_TILE, BRANCH_WIDTH), lambda bb, i: (bb, i + first, j))
    prev = lambda j: pl.BlockSpec(
        (None, HALO, BRANCH_WIDTH), lambda bb, i: (bb, jnp.maximum((i + first) * hb - 1, 0), j))
    nxt = lambda j: pl.BlockSpec(
        (None, HALO, BRANCH_WIDTH), lambda bb, i: (bb, jnp.minimum((i + first + 1) * hb, t // HALO - 1), j))
    branch = pl.BlockSpec((None, TOKEN_TILE, BRANCH_WIDTH), lambda bb, i: (bb, i, 0))
    return pl.pallas_call(
        functools.partial(_merge_kernel, n_stream=len(streams), first=first, n_tiles=n_tiles, layer=layer),
        grid=(b, n_tiles - first),
        in_specs=_stream_specs(len(streams) == 2, D_MODEL, first)
        + _stream_specs(len(ya) == 2, BRANCH_WIDTH, 0) + [
            pl.BlockSpec((None, TOKEN_TILE, D_MODEL), lambda bb, i: (bb, i + first, 0)),
            col(COL_BG), col(COL_CG), col(COL_XS), col(COL_P),
            prev(COL_CG), prev(COL_XS), prev(COL_P), nxt(COL_CG), nxt(COL_XS), nxt(COL_P),
            branch,
            _mod_spec(first),
            _const_spec((CONV_WIDTH, BRANCH_WIDTH)),
            _const_spec((1, BRANCH_WIDTH)),
            _layer_spec((len(POOL_WINDOWS), POOL_GROUP, POOL_GROUP), layer),
            _const_spec((1, BRANCH_WIDTH)),
            _HBM,
            _const_spec((1, N_BRANCH * D_MODEL)),
            _HBM,
            _HBM,
        ],
        out_specs=pl.BlockSpec((None, TOKEN_TILE, D_MODEL), lambda bb, i: (bb, i, 0)),
        out_shape=jax.ShapeDtypeStruct((b, (n_tiles - first) * TOKEN_TILE, D_MODEL), F32),
        scratch_shapes=[
            pltpu.VMEM((TOKEN_TILE + 2 * HALO, BRANCH_WIDTH), F32),
            pltpu.VMEM((TOKEN_TILE + 2 * HALO, BRANCH_WIDTH), F32),
            pltpu.VMEM((D_MODEL, N_BRANCH * D_MODEL), BF16),
            pltpu.VMEM((N_BRANCH, BRANCH_WIDTH, D_MODEL), BF16),
            pltpu.VMEM((D_MODEL, D_MODEL), BF16),
            pltpu.VMEM((STREAM_SLOTS, STAGE_ROWS, D_MODEL), F32),
            pltpu.SemaphoreType.DMA((STREAM_SLOTS,)),
        ],
        compiler_params=_params("arbitrary", "arbitrary"),
        name="merge",
    )(*streams, *ya, h, u, u, u, u, u, u, u, u, u, u, yf, modsel, conv_w, conv_b, w_pool,
      pool_scale, w_gate, b_gate, w_br, w_out)


def _ffn_kernel(*refs, final, layer):
    x_ref = refs[0]
    mod_refs = refs[1:1 + PAIR]
    (g_ref, w1_hbm, b1_ref, w2_hbm, b2_ref, fg_ref, o_ref, w1_ref, w2_ref, stage, sem) = refs[1 + PAIR:]
    tiles = [(slice(k * TOKEN_TILE, (k + 1) * TOKEN_TILE), mod_refs[k]) for k in range(PAIR)]
    rows_per_chunk = stage.shape[1]
    halves = D_MODEL // rows_per_chunk

    def weight_chunks():
        chunks = []
        for c in range(D_FF // D_MODEL):
            cols = slice(c * D_MODEL, (c + 1) * D_MODEL)
            for r in range(halves):
                rows = slice(r * rows_per_chunk, (r + 1) * rows_per_chunk)
                chunks.append((w1_hbm.at[layer, rows, cols], w1_ref.at[rows, cols]))
            for r in range(halves):
                rows = slice(c * D_MODEL + r * rows_per_chunk, c * D_MODEL + (r + 1) * rows_per_chunk)
                chunks.append((w2_hbm.at[layer, rows, :], w2_ref.at[rows, :]))
        return chunks

    def normed(rows, mod_ref):
        shift = mod_ref[:, 3 * D_MODEL:4 * D_MODEL]
        scale = mod_ref[:, 4 * D_MODEL:5 * D_MODEL]
        return (_rms(x_ref[rows, :]) * g_ref[...] * (1.0 + scale) + shift).astype(BF16)

    def step(need):
        hb = jnp.concatenate([normed(rows, mod_ref) for rows, mod_ref in tiles], axis=0)
        acc = jnp.zeros(x_ref.shape, F32)
        for c in range(D_FF // D_MODEL):
            cols = slice(c * D_MODEL, (c + 1) * D_MODEL)
            need((2 * c + 1) * halves)
            hid = jnp.square(jnp.maximum(_dot(hb, w1_ref[:, cols]) + b1_ref[:, cols], 0.0))
            need((2 * c + 2) * halves)
            acc = acc + _dot(hid.astype(BF16), w2_ref[cols, :])
        for rows, mod_ref in tiles:
            gate = mod_ref[:, 5 * D_MODEL:6 * D_MODEL]
            y = x_ref[rows, :] + gate * (acc[rows] + b2_ref[...])
            if final:
                y = _rms(y) * fg_ref[...]
            o_ref[rows, :] = y

    _first_step_or_later(pl.program_id(0) == 0, lambda: _WeightStream(weight_chunks(), stage, sem), step)


def _ffn(xs, modsel, gain, w1, b1, w2, b2, final_g, layer, latent_only, final):
    b, rows, _ = xs.shape
    nt = rows // TOKEN_TILE
    first = 1 if latent_only else 0
    assert (b * nt) % PAIR == 0

    def mod(k):
        def index(s):
            tile = s * PAIR + k
            return (tile // nt, jnp.minimum(tile % nt + first, 1), 0, 0)
        return pl.BlockSpec((None, None, 1, N_MOD * D_MODEL), index)

    block = pl.BlockSpec((PAIR * TOKEN_TILE, D_MODEL), lambda s: (s, 0))
    out = pl.pallas_call(
        functools.partial(_ffn_kernel, final=final, layer=layer),
        grid=(b * nt // PAIR,),
        in_specs=[block] + [mod(k) for k in range(PAIR)] + [
            _const_spec((1, D_MODEL)),
            _HBM,
            _const_spec((1, D_FF)),
            _HBM,
            _const_spec((1, D_MODEL)),
            _const_spec((1, D_MODEL)),
        ],
        out_specs=block,
        out_shape=jax.ShapeDtypeStruct((b * rows, D_MODEL), F32),
        scratch_shapes=[
            pltpu.VMEM((D_MODEL, D_FF), BF16),
            pltpu.VMEM((D_FF, D_MODEL), BF16),
            pltpu.VMEM((STREAM_SLOTS, STAGE_ROWS, D_MODEL), F32),
            pltpu.SemaphoreType.DMA((STREAM_SLOTS,)),
        ],
        compiler_params=_params("arbitrary"),
        name="ffn",
    )(xs.reshape(b * rows, D_MODEL), *([modsel] * PAIR), gain, w1, b1, w2, b2, final_g)
    return out.reshape(b, rows, D_MODEL)


def _rope_tables(seq):
    lane = np.arange(LANES)
    d = lane % A_HEAD_DIM
    axis = d // AXIS_ROT
    upper = (d % AXIS_ROT) // (AXIS_ROT // 2)
    inv = ROPE_BASE ** (-(d % (AXIS_ROT // 2)) * 2.0 / AXIS_ROT)
    tok = np.arange(seq)
    pos = np.where(axis[None, :] == 0, (tok // GRID_W)[:, None], (tok % GRID_W)[:, None])
    ang = pos * inv[None, :]
    cos, sin = np.cos(ang), np.sin(ang)
    s_up = np.where(upper[None, :] == 0, -sin, 0.0)
    s_dn = np.where(upper[None, :] == 1, sin, 0.0)
    pad = lambda a, v: jnp.asarray(np.concatenate([np.full((CTX_LEN, LANES), v), a], axis=0), F32)
    return pad(cos, 1.0), pad(s_up, 0.0), pad(s_dn, 0.0)


def _dft_cos_sin(n, rows, cols):
    ang = (np.arange(rows)[:, None] * np.arange(cols)[None, :] % n) * (2.0 * math.pi / n)
    return jnp.asarray(np.cos(ang) * n ** -0.5, F32), jnp.asarray(np.sin(ang) * n ** -0.5, F32)


def kernel(x, c, ctx, c_ctx, w_mod, b_mod, norm1_g, w_in, lam_q1, lam_k1, lam_q2, lam_k2, subln_g, conv_w,
           conv_b, w_pool, pool_scale, w_gate, b_gate, w_br, w_out, norm2_g, w_ff1, b_ff1, w_ff2, b_ff2,
           final_g):
    batch, seq, d_model = x.shape
    depth = w_mod.shape[0]
    assert d_model == D_MODEL and ctx.shape[1] == CTX_LEN == TOKEN_TILE
    assert seq % (2 * REV_BLOCK) == 0 and batch + 1 <= COND_ROWS

    cond = jnp.concatenate([c, c_ctx[None, :], jnp.zeros((COND_ROWS - batch - 1, D_MODEL), F32)], axis=0)
    mods = _adaln(cond, w_mod, b_mod)

    rope = _rope_tables(seq)
    ch, sh = (a.astype(BF16) for a in _dft_cos_sin(seq, seq // 2, seq // 2))
    cc, sc = _dft_cos_sin(CTX_LEN, CTX_LEN, CTX_LEN)
    wc = jnp.concatenate([cc, -sc], axis=1).astype(BF16)
    cg, sg = _dft_cos_sin(FOURIER_GROUP, FOURIER_GROUP, FOURIER_GROUP)
    csg = jnp.concatenate([cg, sg], axis=1).astype(BF16)
    rev = jnp.asarray(np.eye(REV_BLOCK)[::-1], F32).astype(BF16)

    w_pool = w_pool.astype(BF16)

    streams = (ctx, x)
    row = lambda a: a.reshape(1, -1)
    for l in range(depth):
        last = l == depth - 1
        lam_init = 0.8 - 0.6 * math.exp(-0.3 * l)
        m = mods[l]
        modsel = jnp.stack([jnp.broadcast_to(m[batch], (batch, N_MOD * D_MODEL)), m[:batch]],
                           axis=1)[:, :, None, :]
        lamvec = jnp.pad(jnp.stack([lam_q1[l], lam_k1[l], lam_q2[l], lam_k2[l]]),
                         ((0, SUBLANES - 4), (0, LANES - A_HEAD_DIM)))

        if len(streams) == 2:
            u, h = _inproj_split(*streams, modsel, row(norm1_g[l]), w_in, l, rope)
        else:
            u, h = _inproj_pair(*streams, modsel, row(norm1_g[l]), w_in, l, rope)
        ya = _attention(u, lamvec, row(jnp.tile(subln_g[l], A_HEADS)), lam_init, ctx_tile=not last)
        yf = _fourier(u, ch, sh, rev, wc, csg, ctx_tile=not last)
        xs = _merge(streams, h, u, ya, yf, modsel, conv_w[l], row(conv_b[l]), w_pool, row(pool_scale[l]),
                    w_gate, row(b_gate[l]), w_br, w_out, l, ctx_tile=not last)
        xs = _ffn(xs, modsel, row(norm2_g[l]), w_ff1, row(b_ff1[l]), w_ff2, row(b_ff2[l]), row(final_g),
                  l, latent_only=last, final=last)
        streams = (xs,)
    return xs
```

```python
import functools
import math

import jax
import jax.numpy as jnp
import numpy as np
from jax import lax
from jax.experimental import pallas as pl
from jax.experimental.pallas import tpu as pltpu

F32 = jnp.float32
BF16 = jnp.bfloat16

D_MODEL = 1024
CTX_LEN = 256
GRID_W = 64
N_BRANCH = 4
BRANCH_WIDTH = 512
A_HEADS = 4
A_HEAD_DIM = 64
HEAD_COLS = 2 * A_HEAD_DIM
AXIS_ROT = A_HEAD_DIM // 2
ROPE_BASE = 10000.0
ATTN_SCALE = A_HEAD_DIM ** -0.5
CONV_WIDTH = 3
FOURIER_GROUPS = 4
FOURIER_GROUP = BRANCH_WIDTH // FOURIER_GROUPS
POOL_WINDOWS = (2, 4, 8, 16)
POOL_GROUP = BRANCH_WIDTH // len(POOL_WINDOWS)
N_MOD = 6
NORM_EPS = 1e-6
IN_WIDTH = 8 * BRANCH_WIDTH
COL_Q, COL_K, COL_V, COL_BG, COL_CG, COL_XS, COL_F, COL_P = range(IN_WIDTH // BRANCH_WIDTH)
D_FF = 4 * D_MODEL

TOKEN_TILE = 256
PAIR = 2
WIDE_HEADS, JOINT_HEADS = (0, 1), (2, 3)
HALO = 16
COND_ROWS = 16
LANES = 128
SUBLANES = 8
REV_BLOCK = 256
LOG2_E = math.log2(math.e)
VMEM_LIMIT = 52 * 1024 * 1024


def _dot(a, b):
    return jnp.dot(a, b, preferred_element_type=F32)


def _rms(x):
    return x * lax.rsqrt(jnp.mean(x * x, axis=-1, keepdims=True) + NORM_EPS)


def _params(*sem):
    return pltpu.CompilerParams(dimension_semantics=sem, vmem_limit_bytes=VMEM_LIMIT)


def _const_spec(shape):
    zeros = (0,) * len(shape)
    return pl.BlockSpec(shape, lambda *_: zeros)


def _layer_spec(shape, layer):
    zeros = (0,) * len(shape)
    return pl.BlockSpec((None,) + shape, lambda *_: (layer,) + zeros)


def _mod_spec(first):
    return pl.BlockSpec((None, None, 1, N_MOD * D_MODEL), lambda bb, i: (bb, jnp.minimum(i + first, 1), 0, 0))


def _stream_specs(split, width, first):
    if split:
        return [pl.BlockSpec((None, TOKEN_TILE, width), lambda bb, i: (bb, 0, 0)),
                pl.BlockSpec((None, TOKEN_TILE, width), lambda bb, i: (bb, jnp.maximum(i + first - 1, 0), 0))]
    return [pl.BlockSpec((None, TOKEN_TILE, width), lambda bb, i: (bb, i + first, 0))]


def _stream_tile(refs, i):
    if len(refs) == 2:
        return jnp.where(i == 0, refs[0][...], refs[1][...])
    return refs[0][...]


STAGE_ROWS = 512
STREAM_SLOTS = 3
_HBM = pl.BlockSpec(memory_space=pl.ANY)


class _WeightStream:
    def __init__(self, chunks, stage, sem):
        self.slots = stage.shape[0]
        self.stage = stage
        self.dsts = [dst for _, dst in chunks]
        self.copies = [pltpu.make_async_copy(src, stage.at[k % self.slots], sem.at[k % self.slots])
                       for k, (src, _) in enumerate(chunks)]
        self.ready = 0
        for copy in self.copies[:self.slots]:
            copy.start()

    def need(self, n):
        while self.ready < n:
            k = self.ready
            self.copies[k].wait()
            self.dsts[k][...] = self.stage[k % self.slots].astype(BF16)
            if k + self.slots < len(self.copies):
                self.copies[k + self.slots].start()
            self.ready += 1


def _first_step_or_later(first_step, make_stream, step):
    pl.when(first_step)(lambda: step(make_stream().need))
    pl.when(jnp.logical_not(first_step))(lambda: step(lambda n: None))


def _adaln_kernel(cond_ref, w_ref, b_ref, o_ref):
    c = cond_ref[...]
    s = c * jax.nn.sigmoid(c)
    o_ref[...] = _dot(s.astype(BF16), w_ref[...].astype(BF16)) + b_ref[...]


def _adaln(cond, w_mod, b_mod):
    depth = w_mod.shape[0]
    return pl.pallas_call(
        _adaln_kernel,
        grid=(depth, N_MOD),
        in_specs=[
            pl.BlockSpec((COND_ROWS, D_MODEL), lambda l, j: (0, 0)),
            pl.BlockSpec((None, D_MODEL, D_MODEL), lambda l, j: (l, 0, j)),
            pl.BlockSpec((None, 1, D_MODEL), lambda l, j: (l, 0, j)),
        ],
        out_specs=pl.BlockSpec((None, COND_ROWS, D_MODEL), lambda l, j: (l, 0, j)),
        out_shape=jax.ShapeDtypeStruct((depth, COND_ROWS, N_MOD * D_MODEL), F32),
        compiler_params=_params("arbitrary", "arbitrary"),
        name="adaln",
    )(cond, w_mod, b_mod.reshape(depth, 1, N_MOD * D_MODEL))


def _inproj_body(x_tiles, mod_refs, rope_refs, g_ref, w_ref, u_ref, h_ref, need):
    hs = []
    for x, mod_ref in zip(x_tiles, mod_refs):
        shift = mod_ref[:, 0:D_MODEL]
        scale = mod_ref[:, D_MODEL:2 * D_MODEL]
        hs.append((_rms(x) * g_ref[...] * (1.0 + scale) + shift).astype(BF16))
    hb = jnp.concatenate(hs, axis=0)
    h_ref[...] = hb
    for j in range(IN_WIDTH // BRANCH_WIDTH):
        cols = slice(j * BRANCH_WIDTH, (j + 1) * BRANCH_WIDTH)
        need(j + 1)
        u = _dot(hb, w_ref[:, cols])
        if j in (COL_Q, COL_K):
            tiles = []
            for k, (cos_ref, sup_ref, sdn_ref) in enumerate(rope_refs):
                parts = []
                for c in range(BRANCH_WIDTH // LANES):
                    uc = u[k * TOKEN_TILE:(k + 1) * TOKEN_TILE, c * LANES:(c + 1) * LANES]
                    up = pltpu.roll(uc, LANES - AXIS_ROT // 2, axis=1)
                    dn = pltpu.roll(uc, AXIS_ROT // 2, axis=1)
                    parts.append(uc * cos_ref[...] + up * sup_ref[...] + dn * sdn_ref[...])
                tiles.append(jnp.concatenate(parts, axis=1))
            u = jnp.concatenate(tiles, axis=0)
            if j == COL_Q:
                u = u * (ATTN_SCALE * LOG2_E)
        u_ref[:, cols] = u.astype(BF16)


def _inproj_stream(w_hbm, w_ref, stage, sem, layer):
    blocks = [slice(j * BRANCH_WIDTH, (j + 1) * BRANCH_WIDTH) for j in range(IN_WIDTH // BRANCH_WIDTH)]
    return _WeightStream([(w_hbm.at[layer, :, cols], w_ref.at[:, cols]) for cols in blocks], stage, sem)


def _inproj_split_kernel(ctx_ref, x_ref, mod_ref, g_ref, w_hbm, cos_ref, sup_ref, sdn_ref, u_ref, h_ref,
                         w_ref, stage, sem, *, layer):
    def step(need):
        x = _stream_tile((ctx_ref, x_ref), pl.program_id(1))
        _inproj_body([x], [mod_ref], [(cos_ref, sup_ref, sdn_ref)], g_ref, w_ref, u_ref, h_ref, need)

    first_step = jnp.logical_and(pl.program_id(0) == 0, pl.program_id(1) == 0)
    _first_step_or_later(first_step, lambda: _inproj_stream(w_hbm, w_ref, stage, sem, layer), step)


def _inproj_pair_kernel(*refs, layer):
    x_ref = refs[0]
    mod_refs = refs[1:1 + PAIR]
    g_ref, w_hbm = refs[1 + PAIR:3 + PAIR]
    rope_refs = [refs[3 + PAIR + 3 * k:6 + PAIR + 3 * k] for k in range(PAIR)]
    u_ref, h_ref, w_ref, stage, sem = refs[3 + 4 * PAIR:]

    def step(need):
        x_tiles = [x_ref[k * TOKEN_TILE:(k + 1) * TOKEN_TILE, :] for k in range(PAIR)]
        _inproj_body(x_tiles, mod_refs, rope_refs, g_ref, w_ref, u_ref, h_ref, need)

    _first_step_or_later(pl.program_id(0) == 0, lambda: _inproj_stream(w_hbm, w_ref, stage, sem, layer), step)


_INPROJ_SCRATCH = [pltpu.VMEM((D_MODEL, IN_WIDTH), BF16),
                   pltpu.VMEM((STREAM_SLOTS, D_MODEL, BRANCH_WIDTH), F32),
                   pltpu.SemaphoreType.DMA((STREAM_SLOTS,))]


def _inproj_split(ctx, x, modsel, gain, w_in, layer, rope):
    b = x.shape[0]
    t = rope[0].shape[0]
    tile = lambda bb, i: (bb, i, 0)
    rope_spec = pl.BlockSpec((TOKEN_TILE, LANES), lambda bb, i: (i, 0))
    return pl.pallas_call(
        functools.partial(_inproj_split_kernel, layer=layer),
        grid=(b, t // TOKEN_TILE),
        in_specs=_stream_specs(True, D_MODEL, 0) + [
            _mod_spec(0),
            _const_spec((1, D_MODEL)),
            _HBM,
            rope_spec, rope_spec, rope_spec,
        ],
        out_specs=[
            pl.BlockSpec((None, TOKEN_TILE, IN_WIDTH), tile),
            pl.BlockSpec((None, TOKEN_TILE, D_MODEL), tile),
        ],
        out_shape=[
            jax.ShapeDtypeStruct((b, t, IN_WIDTH), BF16),
            jax.ShapeDtypeStruct((b, t, D_MODEL), BF16),
        ],
        scratch_shapes=_INPROJ_SCRATCH,
        compiler_params=_params("arbitrary", "arbitrary"),
        name="inproj",
    )(ctx, x, modsel, gain, w_in, *rope)


def _inproj_pair(xs, modsel, gain, w_in, layer, rope):
    b, t, _ = xs.shape
    nt = t // TOKEN_TILE
    assert (b * nt) % PAIR == 0

    def mod(k):
        def index(s):
            tile = s * PAIR + k
            return (tile // nt, jnp.minimum(tile % nt, 1), 0, 0)
        return pl.BlockSpec((None, None, 1, N_MOD * D_MODEL), index)

    def rope_specs(k):
        return [pl.BlockSpec((TOKEN_TILE, LANES), lambda s: ((s * PAIR + k) % nt, 0))] * 3

    rows = lambda width: pl.BlockSpec((PAIR * TOKEN_TILE, width), lambda s: (s, 0))
    u, h = pl.pallas_call(
        functools.partial(_inproj_pair_kernel, layer=layer),
        grid=(b * nt // PAIR,),
        in_specs=[rows(D_MODEL)] + [mod(k) for k in range(PAIR)] + [
            _const_spec((1, D_MODEL)),
            _HBM,
        ] + [spec for k in range(PAIR) for spec in rope_specs(k)],
        out_specs=[rows(IN_WIDTH), rows(D_MODEL)],
        out_shape=[
            jax.ShapeDtypeStruct((b * t, IN_WIDTH), BF16),
            jax.ShapeDtypeStruct((b * t, D_MODEL), BF16),
        ],
        scratch_shapes=_INPROJ_SCRATCH,
        compiler_params=_params("arbitrary"),
        name="inproj",
    )(xs.reshape(b * t, D_MODEL), *([modsel] * PAIR), gain, w_in, *(rope * PAIR))
    return u.reshape(b, t, IN_WIDTH), h.reshape(b, t, D_MODEL)


def _attn_kernel(*refs, lam_init, ctx_tile):
    q_refs = refs[:PAIR]
    k_ref, v_ref, lam_ref, g_ref = refs[PAIR:PAIR + 4]
    out_refs, vx_ref = refs[PAIR + 4:-1], refs[-1]
    lv = lam_ref[...]
    lam = (jnp.exp(jnp.sum(lv[0:1] * lv[1:2], axis=1, keepdims=True))
           - jnp.exp(jnp.sum(lv[2:3] * lv[3:4], axis=1, keepdims=True)) + lam_init)
    t = k_ref.shape[0]

    @pl.when(pl.program_id(1) == 0)
    def _():
        for w, h in enumerate(WIDE_HEADS):
            vx_ref[:, 2 * w * HEAD_COLS:(2 * w + 1) * HEAD_COLS] = v_ref[:, h * HEAD_COLS:(h + 1) * HEAD_COLS]
            vx_ref[:, (2 * w + 1) * HEAD_COLS:(2 * w + 2) * HEAD_COLS] = jnp.ones((t, HEAD_COLS), BF16)

    def run(q_refs, n_keys, o_ref):
        rows = len(q_refs) * TOKEN_TILE
        lane = lax.broadcasted_iota(jnp.int32, (rows, HEAD_COLS), 1)
        zero = jnp.zeros((rows, HEAD_COLS), BF16)

        def scores(chain):
            h, m = divmod(chain, 2)
            cols = slice(h * HEAD_COLS, (h + 1) * HEAD_COLS)
            q = jnp.concatenate([q_ref[:, cols] for q_ref in q_refs], axis=0)
            qm = jnp.where((lane >= A_HEAD_DIM) if m else (lane < A_HEAD_DIM), q, zero)
            return lax.dot_general(qm, k_ref[0:n_keys, cols], (((1,), (1,)), ((), ())),
                                   preferred_element_type=F32)

        units = []
        for hw, hj in zip(WIDE_HEADS, JOINT_HEADS):
            units += [("joint", hj), ("wide", hw, 0), ("wide", hw, 1)]

        def issue(unit):
            return [scores(2 * unit[1] + m) for m in ((unit[2],) if unit[0] == "wide" else (0, 1))]

        def parts(s):
            p = jnp.exp2(s - jnp.max(s, axis=-1, keepdims=True))
            return p, jnp.sum(p, axis=-1, keepdims=True)

        wide, heads = {}, {}
        s_next = issue(units[0])
        for n, unit in enumerate(units):
            s = s_next
            if n + 1 < len(units):
                s_next = issue(units[n + 1])
            h = unit[1]
            if unit[0] == "wide":
                p = jnp.exp2(s[0] - jnp.max(s[0], axis=-1, keepdims=True)).astype(BF16)
                w = WIDE_HEADS.index(h)
                wide[h, unit[2]] = _dot(p, vx_ref[0:n_keys, 2 * w * HEAD_COLS:(2 * w + 2) * HEAD_COLS])
            else:
                (p1, l1), (p2, l2) = parts(s[0]), parts(s[1])
                a = (p1 * (1.0 / l1) - p2 * (lam / l2)).astype(BF16)
                heads[h] = _dot(a, v_ref[0:n_keys, h * HEAD_COLS:(h + 1) * HEAD_COLS])
        for (h, m), r in wide.items():
            if m == 0:
                r2 = wide[h, 1]
                heads[h] = r[:, :HEAD_COLS] / r[:, HEAD_COLS:] - lam * (r2[:, :HEAD_COLS] / r2[:, HEAD_COLS:])
        outs = [_rms(heads[h]) for h in range(A_HEADS)]
        y = jnp.concatenate(outs, axis=1) * g_ref[...] * (1.0 - lam_init)
        o_ref[...] = y.astype(BF16)

    if ctx_tile:
        pl.when(pl.program_id(1) == 0)(lambda: run(q_refs[:1], CTX_LEN, out_refs[0]))
        pl.when(pl.program_id(1) > 0)(lambda: run(q_refs, t, out_refs[1]))
    else:
        run(q_refs, t, out_refs[0])


def _attention(u, lamvec, subln_g, lam_init, ctx_tile):
    b, t, _ = u.shape
    n_pairs = (t - CTX_LEN) // (PAIR * TOKEN_TILE)
    lead = 1 if ctx_tile else 0

    def q_spec(k):
        def index(bb, i):
            return (bb, jnp.maximum((i - lead) * PAIR + 1 + k, k), 0)
        return pl.BlockSpec((None, TOKEN_TILE, BRANCH_WIDTH), index)

    latent = pl.BlockSpec((None, PAIR * TOKEN_TILE, BRANCH_WIDTH), lambda bb, i: (bb, jnp.maximum(i - lead, 0), 0))
    latent_shape = jax.ShapeDtypeStruct((b, t - CTX_LEN, BRANCH_WIDTH), BF16)
    context = pl.BlockSpec((None, CTX_LEN, BRANCH_WIDTH), lambda bb, i: (bb, 0, 0))
    context_shape = jax.ShapeDtypeStruct((b, CTX_LEN, BRANCH_WIDTH), BF16)
    return pl.pallas_call(
        functools.partial(_attn_kernel, lam_init=lam_init, ctx_tile=ctx_tile),
        grid=(b, n_pairs + lead),
        in_specs=[q_spec(k) for k in range(PAIR)] + [
            pl.BlockSpec((None, t, BRANCH_WIDTH), lambda bb, i: (bb, 0, COL_K)),
            pl.BlockSpec((None, t, BRANCH_WIDTH), lambda bb, i: (bb, 0, COL_V)),
            _const_spec((SUBLANES, LANES)),
            _const_spec((1, BRANCH_WIDTH)),
        ],
        out_specs=[context, latent] if ctx_tile else [latent],
        out_shape=[context_shape, latent_shape] if ctx_tile else [latent_shape],
        scratch_shapes=[pltpu.VMEM((t, 2 * HEAD_COLS * len(WIDE_HEADS)), BF16)],
        compiler_params=_params("arbitrary", "arbitrary"),
        name="diff_attention",
    )(*([u] * PAIR), u, u, lamvec, subln_g)


def _fourier_kernel(f_ref, ch_ref, sh_ref, rev_ref, wc_ref, csg_ref, o_ref, buf, *, ctx_tile):
    t = f_ref.shape[0]
    seq = t - CTX_LEN
    m = seq // 2
    out0 = CTX_LEN if ctx_tile else 0

    def channel_dft(rows):
        zc, zs = [], []
        for g in range(FOURIER_GROUPS):
            zz = _dot(f_ref[rows, g * FOURIER_GROUP:(g + 1) * FOURIER_GROUP], csg_ref[...])
            zc.append(zz[:, :FOURIER_GROUP])
            zs.append(zz[:, FOURIER_GROUP:])
        return jnp.concatenate(zc, axis=1), jnp.concatenate(zs, axis=1)

    def reversed_shifted(a, row0):
        ab = a.astype(BF16)
        nb = m // REV_BLOCK
        for c in range(nb):
            buf[SUBLANES + c * REV_BLOCK:SUBLANES + (c + 1) * REV_BLOCK, :] = _dot(
                rev_ref[...], ab[(nb - 1 - c) * REV_BLOCK:(nb - c) * REV_BLOCK, :])
        buf[SUBLANES - 1:SUBLANES, :] = row0
        return buf[SUBLANES - 1:SUBLANES - 1 + m, :]

    if ctx_tile:
        zc, zs = channel_dft(slice(0, CTX_LEN))
        o_ref[0:CTX_LEN, :] = (_dot(wc_ref[:, :CTX_LEN], zc.astype(BF16))
                               + _dot(wc_ref[:, CTX_LEN:], zs.astype(BF16))).astype(BF16)

    zc, zs = channel_dft(slice(CTX_LEN, t))
    zero_row = jnp.zeros((1, BRANCH_WIDTH), F32)
    e = zc[:m] + reversed_shifted(zc[m:], zero_row)
    o = zs[:m] - reversed_shifted(zs[m:], zero_row)
    row = lax.broadcasted_iota(jnp.int32, (m, BRANCH_WIDTH), 0)
    sign = jnp.where((row & 1) == 0, 1.0, -1.0)
    nyq = zc[m:m + 1] * seq ** -0.5
    p = _dot(ch_ref[...], e.astype(BF16)) + sign * nyq
    q = _dot(sh_ref[...], o.astype(BF16))
    o_ref[out0:out0 + m, :] = (p - q).astype(BF16)
    y_mid = jnp.sum(sign * e, axis=0, keepdims=True) * seq ** -0.5 + nyq
    o_ref[out0 + m:out0 + seq, :] = reversed_shifted(p + q, y_mid).astype(BF16)


def _fourier(u, ch, sh, rev, wc, csg, ctx_tile):
    b, t, _ = u.shape
    seq = t - CTX_LEN
    out_t = t if ctx_tile else seq
    return pl.pallas_call(
        functools.partial(_fourier_kernel, ctx_tile=ctx_tile),
        grid=(b,),
        in_specs=[
            pl.BlockSpec((None, t, BRANCH_WIDTH), lambda bb: (bb, 0, COL_F)),
            _const_spec((seq // 2, seq // 2)),
            _const_spec((seq // 2, seq // 2)),
            _const_spec((REV_BLOCK, REV_BLOCK)),
            _const_spec((CTX_LEN, 2 * CTX_LEN)),
            _const_spec((FOURIER_GROUP, 2 * FOURIER_GROUP)),
        ],
        out_specs=pl.BlockSpec((None, out_t, BRANCH_WIDTH), lambda bb: (bb, 0, 0)),
        out_shape=jax.ShapeDtypeStruct((b, out_t, BRANCH_WIDTH), BF16),
        scratch_shapes=[pltpu.VMEM((seq // 2 + 2 * SUBLANES, BRANCH_WIDTH), F32)],
        compiler_params=_params("arbitrary"),
        name="fourier",
    )(u, ch, sh, rev, wc, csg)


def _merge_kernel(*refs, n_stream, first, n_tiles, layer):
    x_refs, ya_refs = refs[:n_stream], refs[n_stream:2 * n_stream]
    (h_ref, bg_ref, cg_ref, xs_ref, p_ref, cgp_ref, xsp_ref, pp_ref, cgn_ref, xsn_ref, pn_ref,
     yf_ref, mod_ref, convw_ref, convb_ref,
     wgate_hbm, bgate_ref, wbr_hbm, wbrp_hbm, wout_hbm, o_ref,
     zbuf, pbuf, wgate_ref, wbr_ref, wout_ref, stage, sem) = refs[2 * n_stream:]
    pool_branch = N_BRANCH - 1
    rows_per_chunk = stage.shape[1]
    halves = D_MODEL // rows_per_chunk
    assert rows_per_chunk == BRANCH_WIDTH

    def square_chunks(w_hbm, w_ref, cols):
        return [(w_hbm.at[layer, r * rows_per_chunk:(r + 1) * rows_per_chunk, cols],
                 w_ref.at[r * rows_per_chunk:(r + 1) * rows_per_chunk, cols]) for r in range(halves)]

    def weight_chunks():
        chunks = []
        for n in range(N_BRANCH):
            chunks += square_chunks(wgate_hbm, wgate_ref, slice(n * D_MODEL, (n + 1) * D_MODEL))
            chunks.append((wbrp_hbm.at[layer] if n == pool_branch else wbr_hbm.at[layer, n], wbr_ref.at[n]))
        return chunks + square_chunks(wout_hbm, wout_ref, slice(0, D_MODEL))

    i = pl.program_id(1) + first
    tm = TOKEN_TILE
    prev_ok = i >= 2
    next_ok = jnp.logical_and(i >= 1, i < n_tiles - 1)

    def step(need):
        z = cg_ref[...].astype(F32) * xs_ref[...].astype(F32)
        zp = cgp_ref[...].astype(F32) * xsp_ref[...].astype(F32)
        zn = cgn_ref[...].astype(F32) * xsn_ref[...].astype(F32)
        zbuf[0:HALO, :] = jnp.where(prev_ok, zp, 0.0)
        zbuf[HALO:HALO + tm, :] = z
        zbuf[HALO + tm:, :] = jnp.where(next_ok, zn, 0.0)
        conv = (zbuf[HALO - 1:HALO - 1 + tm, :] * convw_ref[0:1, :] + z * convw_ref[1:2, :]
                + zbuf[HALO + 1:HALO + 1 + tm, :] * convw_ref[2:3, :] + convb_ref[...])
        y_conv = (bg_ref[...].astype(F32) * conv).astype(BF16)

        p = p_ref[...].astype(F32)
        pbuf[0:HALO, :] = jnp.where(prev_ok, pp_ref[...].astype(F32), 0.0)
        pbuf[HALO:HALO + tm, :] = p
        pbuf[HALO + tm:, :] = jnp.where(next_ok, pn_ref[...].astype(F32), 0.0)
        pos = (lax.broadcasted_iota(jnp.int32, (tm, POOL_GROUP), 0)
               + jnp.where(i == 0, 0, (i - 1) * tm))
        seq_len = jnp.where(i == 0, CTX_LEN, (n_tiles - 1) * tm)
        pool_parts = []
        for g, w in enumerate(POOL_WINDOWS):
            cols = slice(g * POOL_GROUP, (g + 1) * POOL_GROUP)
            win = pbuf[HALO - w // 2:HALO - w // 2 + tm, cols]
            for d in range(1, w):
                win = win + pbuf[HALO - w // 2 + d:HALO - w // 2 + d + tm, cols]
            lo = jnp.maximum(pos - w // 2, 0)
            hi = jnp.minimum(pos - w // 2 + w, seq_len)
            pooled = (win / (hi - lo).astype(F32) - p[:, cols]).astype(BF16)
            pool_parts.append(pooled)
        y_pool = jnp.concatenate(pool_parts, axis=1)

        ys = (_stream_tile(ya_refs, i), y_conv, yf_ref[...], y_pool)
        hb = h_ref[...]
        acc = jnp.zeros((tm, D_MODEL), F32)
        for n in range(N_BRANCH):
            cols = slice(n * D_MODEL, (n + 1) * D_MODEL)
            need(n * (halves + 1) + halves)
            gate = jax.nn.sigmoid(_dot(hb, wgate_ref[:, cols]) + bgate_ref[:, cols])
            need((n + 1) * (halves + 1))
            acc = acc + gate * _dot(ys[n], wbr_ref[n])
        need(N_BRANCH * (halves + 1) + halves)
        out = _dot(acc.astype(BF16), wout_ref[...])
        g1 = mod_ref[:, 2 * D_MODEL:3 * D_MODEL]
        o_ref[...] = _stream_tile(x_refs, i) + g1 * out

    first_step = jnp.logical_and(pl.program_id(0) == 0, pl.program_id(1) == 0)
    _first_step_or_later(first_step, lambda: _WeightStream(weight_chunks(), stage, sem), step)


def _compose_pool_kernel(wp_ref, scale_ref, wbr_ref, o_ref):
    o_ref[...] = jnp.dot(wp_ref[...] * scale_ref[...], wbr_ref[...], preferred_element_type=F32,
                         precision=lax.Precision.HIGHEST)


def _compose_pool(w_pool, pool_scale, w_br):
    depth, groups = w_pool.shape[:2]
    return pl.pallas_call(
        _compose_pool_kernel,
        grid=(depth, groups),
        in_specs=[
            pl.BlockSpec((None, None, POOL_GROUP, POOL_GROUP), lambda l, g: (l, g, 0, 0)),
            pl.BlockSpec((None, 1, POOL_GROUP), lambda l, g: (l, 0, g)),
            pl.BlockSpec((None, None, POOL_GROUP, D_MODEL), lambda l, g: (l, N_BRANCH - 1, g, 0)),
        ],
        out_specs=pl.BlockSpec((None, POOL_GROUP, D_MODEL), lambda l, g: (l, g, 0)),
        out_shape=jax.ShapeDtypeStruct((depth, BRANCH_WIDTH, D_MODEL), F32),
        compiler_params=_params("arbitrary", "arbitrary"),
        name="compose_pool",
    )(w_pool, pool_scale.reshape(depth, 1, BRANCH_WIDTH), w_br)


def _merge(streams, h, u, ya, yf, modsel, conv_w, conv_b, w_gate, b_gate, w_br, w_br_pool, w_out,
           layer, ctx_tile):
    assert len(streams) == len(ya)
    b, t, _ = u.shape
    n_tiles = t // TOKEN_TILE
    first = 0 if ctx_tile else 1
    hb = TOKEN_TILE // HALO
    col = lambda j: pl.BlockSpec((None, TOKEN_TILE, BRANCH_WIDTH), lambda bb, i: (bb, i + first, j))
    prev = lambda j: pl.BlockSpec(
        (None, HALO, BRANCH_WIDTH), lambda bb, i: (bb, jnp.maximum((i + first) * hb - 1, 0), j))
    nxt = lambda j: pl.BlockSpec(
        (None, HALO, BRANCH_WIDTH), lambda bb, i: (bb, jnp.minimum((i + first + 1) * hb, t // HALO - 1), j))
    branch = pl.BlockSpec((None, TOKEN_TILE, BRANCH_WIDTH), lambda bb, i: (bb, i, 0))
    return pl.pallas_call(
        functools.partial(_merge_kernel, n_stream=len(streams), first=first, n_tiles=n_tiles, layer=layer),
        grid=(b, n_tiles - first),
        in_specs=_stream_specs(len(streams) == 2, D_MODEL, first)
        + _stream_specs(len(ya) == 2, BRANCH_WIDTH, 0) + [
            pl.BlockSpec((None, TOKEN_TILE, D_MODEL), lambda bb, i: (bb, i + first, 0)),
            col(COL_BG), col(COL_CG), col(COL_XS), col(COL_P),
            prev(COL_CG), prev(COL_XS), prev(COL_P), nxt(COL_CG), nxt(COL_XS), nxt(COL_P),
            branch,
            _mod_spec(first),
            _const_spec((CONV_WIDTH, BRANCH_WIDTH)),
            _const_spec((1, BRANCH_WIDTH)),
            _HBM,
            _const_spec((1, N_BRANCH * D_MODEL)),
            _HBM,
            _HBM,
            _HBM,
        ],
        out_specs=pl.BlockSpec((None, TOKEN_TILE, D_MODEL), lambda bb, i: (bb, i, 0)),
        out_shape=jax.ShapeDtypeStruct((b, (n_tiles - first) * TOKEN_TILE, D_MODEL), F32),
        scratch_shapes=[
            pltpu.VMEM((TOKEN_TILE + 2 * HALO, BRANCH_WIDTH), F32),
            pltpu.VMEM((TOKEN_TILE + 2 * HALO, BRANCH_WIDTH), F32),
            pltpu.VMEM((D_MODEL, N_BRANCH * D_MODEL), BF16),
            pltpu.VMEM((N_BRANCH, BRANCH_WIDTH, D_MODEL), BF16),
            pltpu.VMEM((D_MODEL, D_MODEL), BF16),
            pltpu.VMEM((STREAM_SLOTS, STAGE_ROWS, D_MODEL), F32),
            pltpu.SemaphoreType.DMA((STREAM_SLOTS,)),
        ],
        compiler_params=_params("arbitrary", "arbitrary"),
        name="merge",
    )(*streams, *ya, h, u, u, u, u, u, u, u, u, u, u, yf, modsel, conv_w, conv_b,
      w_gate, b_gate, w_br, w_br_pool, w_out)


def _ffn_kernel(*refs, final, layer):
    x_ref = refs[0]
    mod_refs = refs[1:1 + PAIR]
    (g_ref, w1_hbm, b1_ref, w2_hbm, b2_ref, fg_ref, o_ref, w1_ref, w2_ref, stage, sem) = refs[1 + PAIR:]
    tiles = [(slice(k * TOKEN_TILE, (k + 1) * TOKEN_TILE), mod_refs[k]) for k in range(PAIR)]
    rows_per_chunk = stage.shape[1]
    halves = D_MODEL // rows_per_chunk

    def weight_chunks():
        chunks = []
        for c in range(D_FF // D_MODEL):
            cols = slice(c * D_MODEL, (c + 1) * D_MODEL)
            for r in range(halves):
                rows = slice(r * rows_per_chunk, (r + 1) * rows_per_chunk)
                chunks.append((w1_hbm.at[layer, rows, cols], w1_ref.at[rows, cols]))
            for r in range(halves):
                rows = slice(c * D_MODEL + r * rows_per_chunk, c * D_MODEL + (r + 1) * rows_per_chunk)
                chunks.append((w2_hbm.at[layer, rows, :], w2_ref.at[rows, :]))
        return chunks

    def normed(rows, mod_ref):
        shift = mod_ref[:, 3 * D_MODEL:4 * D_MODEL]
        scale = mod_ref[:, 4 * D_MODEL:5 * D_MODEL]
        return (_rms(x_ref[rows, :]) * g_ref[...] * (1.0 + scale) + shift).astype(BF16)

    def step(need):
        hb = jnp.concatenate([normed(rows, mod_ref) for rows, mod_ref in tiles], axis=0)
        acc = jnp.zeros(x_ref.shape, F32)
        for c in range(D_FF // D_MODEL):
            cols = slice(c * D_MODEL, (c + 1) * D_MODEL)
            need((2 * c + 1) * halves)
            hid = jnp.square(jnp.maximum(_dot(hb, w1_ref[:, cols]) + b1_ref[:, cols], 0.0))
            need((2 * c + 2) * halves)
            acc = acc + _dot(hid.astype(BF16), w2_ref[cols, :])
        for rows, mod_ref in tiles:
            gate = mod_ref[:, 5 * D_MODEL:6 * D_MODEL]
            y = x_ref[rows, :] + gate * (acc[rows] + b2_ref[...])
            if final:
                y = _rms(y) * fg_ref[...]
            o_ref[rows, :] = y

    _first_step_or_later(pl.program_id(0) == 0, lambda: _WeightStream(weight_chunks(), stage, sem), step)


def _ffn(xs, modsel, gain, w1, b1, w2, b2, final_g, layer, latent_only, final):
    b, rows, _ = xs.shape
    nt = rows // TOKEN_TILE
    first = 1 if latent_only else 0
    assert (b * nt) % PAIR == 0

    def mod(k):
        def index(s):
            tile = s * PAIR + k
            return (tile // nt, jnp.minimum(tile % nt + first, 1), 0, 0)
        return pl.BlockSpec((None, None, 1, N_MOD * D_MODEL), index)

    block = pl.BlockSpec((PAIR * TOKEN_TILE, D_MODEL), lambda s: (s, 0))
    out = pl.pallas_call(
        functools.partial(_ffn_kernel, final=final, layer=layer),
        grid=(b * nt // PAIR,),
        in_specs=[block] + [mod(k) for k in range(PAIR)] + [
            _const_spec((1, D_MODEL)),
            _HBM,
            _const_spec((1, D_FF)),
            _HBM,
            _const_spec((1, D_MODEL)),
            _const_spec((1, D_MODEL)),
        ],
        out_specs=block,
        out_shape=jax.ShapeDtypeStruct((b * rows, D_MODEL), F32),
        scratch_shapes=[
            pltpu.VMEM((D_MODEL, D_FF), BF16),
            pltpu.VMEM((D_FF, D_MODEL), BF16),
            pltpu.VMEM((STREAM_SLOTS, STAGE_ROWS, D_MODEL), F32),
            pltpu.SemaphoreType.DMA((STREAM_SLOTS,)),
        ],
        compiler_params=_params("arbitrary"),
        name="ffn",
    )(xs.reshape(b * rows, D_MODEL), *([modsel] * PAIR), gain, w1, b1, w2, b2, final_g)
    return out.reshape(b, rows, D_MODEL)


def _rope_tables(seq):
    lane = np.arange(LANES)
    d = lane % A_HEAD_DIM
    axis = d // AXIS_ROT
    upper = (d % AXIS_ROT) // (AXIS_ROT // 2)
    inv = ROPE_BASE ** (-(d % (AXIS_ROT // 2)) * 2.0 / AXIS_ROT)
    tok = np.arange(seq)
    pos = np.where(axis[None, :] == 0, (tok // GRID_W)[:, None], (tok % GRID_W)[:, None])
    ang = pos * inv[None, :]
    cos, sin = np.cos(ang), np.sin(ang)
    s_up = np.where(upper[None, :] == 0, -sin, 0.0)
    s_dn = np.where(upper[None, :] == 1, sin, 0.0)
    pad = lambda a, v: jnp.asarray(np.concatenate([np.full((CTX_LEN, LANES), v), a], axis=0), F32)
    return pad(cos, 1.0), pad(s_up, 0.0), pad(s_dn, 0.0)


def _dft_cos_sin(n, rows, cols):
    ang = (np.arange(rows)[:, None] * np.arange(cols)[None, :] % n) * (2.0 * math.pi / n)
    return jnp.asarray(np.cos(ang) * n ** -0.5, F32), jnp.asarray(np.sin(ang) * n ** -0.5, F32)


def kernel(x, c, ctx, c_ctx, w_mod, b_mod, norm1_g, w_in, lam_q1, lam_k1, lam_q2, lam_k2, subln_g, conv_w,
           conv_b, w_pool, pool_scale, w_gate, b_gate, w_br, w_out, norm2_g, w_ff1, b_ff1, w_ff2, b_ff2,
           final_g):
    batch, seq, d_model = x.shape
    depth = w_mod.shape[0]
    assert d_model == D_MODEL and ctx.shape[1] == CTX_LEN == TOKEN_TILE
    assert seq % (2 * REV_BLOCK) == 0 and batch + 1 <= COND_ROWS

    cond = jnp.concatenate([c, c_ctx[None, :], jnp.zeros((COND_ROWS - batch - 1, D_MODEL), F32)], axis=0)
    mods = _adaln(cond, w_mod, b_mod)

    rope = _rope_tables(seq)
    ch, sh = (a.astype(BF16) for a in _dft_cos_sin(seq, seq // 2, seq // 2))
    cc, sc = _dft_cos_sin(CTX_LEN, CTX_LEN, CTX_LEN)
    wc = jnp.concatenate([cc, -sc], axis=1).astype(BF16)
    cg, sg = _dft_cos_sin(FOURIER_GROUP, FOURIER_GROUP, FOURIER_GROUP)
    csg = jnp.concatenate([cg, sg], axis=1).astype(BF16)
    rev = jnp.asarray(np.eye(REV_BLOCK)[::-1], F32).astype(BF16)

    w_br_pool = _compose_pool(w_pool, pool_scale, w_br)

    streams = (ctx, x)
    row = lambda a: a.reshape(1, -1)
    for l in range(depth):
        last = l == depth - 1
        lam_init = 0.8 - 0.6 * math.exp(-0.3 * l)
        m = mods[l]
        modsel = jnp.stack([jnp.broadcast_to(m[batch], (batch, N_MOD * D_MODEL)), m[:batch]],
                           axis=1)[:, :, None, :]
        lamvec = jnp.pad(jnp.stack([lam_q1[l], lam_k1[l], lam_q2[l], lam_k2[l]]),
                         ((0, SUBLANES - 4), (0, LANES - A_HEAD_DIM)))

        if len(streams) == 2:
            u, h = _inproj_split(*streams, modsel, row(norm1_g[l]), w_in, l, rope)
        else:
            u, h = _inproj_pair(*streams, modsel, row(norm1_g[l]), w_in, l, rope)
        ya = _attention(u, lamvec, row(jnp.tile(subln_g[l], A_HEADS)), lam_init, ctx_tile=not last)
        yf = _fourier(u, ch, sh, rev, wc, csg, ctx_tile=not last)
        xs = _merge(streams, h, u, ya, yf, modsel, conv_w[l], row(conv_b[l]),
                    w_gate, row(b_gate[l]), w_br, w_br_pool, w_out, l, ctx_tile=not last)
        xs = _ffn(xs, modsel, row(norm2_g[l]), w_ff1, row(b_ff1[l]), w_ff2, row(b_ff2[l]), row(final_g),
                  l, latent_only=last, final=last)
        streams = (xs,)
    return xs
```
